```python
import jax, jax.numpy as jnp
from jax import lax
import numpy as np

D_MODEL = 2048
BATCH = 32
SEQ = 256
DEPTH = 4
DEC_BATCH = 2
DEC_SEQ = 2048
PAST_LEN = 512

GRID_W = 64
HEAD_DIM = 64
C_A = D_MODEL // 2
H_A = C_A // HEAD_DIM
C_B = D_MODEL // 2
DECAY_LORA = 64
ICLR_LORA = 64
GATE_LORA = 160
CONV_K = 31
FFN_CONV_K = 3
D_FF = ((8 * D_MODEL // 3 + 127) // 128) * 128
C_RW = 3 * C_A + 2 * DECAY_LORA + 2 * ICLR_LORA + GATE_LORA
C_IN = C_RW + 2 * C_B + 2 * D_MODEL
ALPHA = (2 * DEPTH) ** 0.25
BETA = (8 * DEPTH) ** -0.25
LN_EPS = 1e-5
GN_EPS = HEAD_DIM * 1e-5

kernel_name = "hybrid_rwkv7_conformer_diffusion_step"


def layer_norm(x, g, b):
    xf = x.astype(jnp.float32)
    mu = jnp.mean(xf, -1, keepdims=True)
    var = jnp.mean(jnp.square(xf - mu), -1, keepdims=True)
    return ((xf - mu) * lax.rsqrt(var + LN_EPS)).astype(x.dtype) * g + b


def dwconv(h, w, b):
    k = w.shape[0]
    out = lax.conv_general_dilated(
        h, w[:, None, :].astype(h.dtype), window_strides=(1,),
        padding=[((k - 1) // 2, (k - 1) // 2)],
        dimension_numbers=("NWC", "WIO", "NWC"),
        feature_group_count=h.shape[-1])
    return out + b


def shift_seq(h):
    prev = jnp.pad(h[:, :-1], ((0, 0), (1, 0), (0, 0)))
    nxt = jnp.pad(h[:, 1:], ((0, 0), (0, 1), (0, 0)))
    even = (jnp.arange(h.shape[-1]) % 2) == 0
    return jnp.where(even, prev, nxt)


def shift_grid(h):
    B, L, C = h.shape
    rows = L // GRID_W
    g = h.reshape(B, rows, GRID_W, C)
    left = jnp.pad(g[:, :, :-1], ((0, 0), (0, 0), (1, 0), (0, 0)))
    right = jnp.pad(g[:, :, 1:], ((0, 0), (0, 0), (0, 1), (0, 0)))
    up = jnp.pad(g[:, :-1], ((0, 0), (1, 0), (0, 0), (0, 0)))
    down = jnp.pad(g[:, 1:], ((0, 0), (0, 1), (0, 0), (0, 0)))
    q = jnp.arange(C) % 4
    out = jnp.where(q == 0, left, jnp.where(q == 1, right, jnp.where(q == 2, up, down)))
    return out.reshape(B, L, C)


def wkv_scan(S0, r, w, k, v, kk, a, reverse):
    def step(S, inp):
        r_t, w_t, k_t, v_t, kk_t, a_t = inp
        s_kk = jnp.einsum("bhvk,bhk->bhv", S, kk_t)
        S = (S * w_t[:, :, None, :]
             - s_kk[..., :, None] * (kk_t * a_t)[:, :, None, :]
             + v_t[..., :, None] * k_t[:, :, None, :])
        return S, jnp.einsum("bhvk,bhk->bhv", S, r_t)
    xs = tuple(jnp.moveaxis(t, 1, 0) for t in (r, w, k, v, kk, a))
    S, ys = lax.scan(step, S0, xs, reverse=reverse)
    return jnp.moveaxis(ys, 0, 1), S


def rwkv_mixer(z, S0, grid, p):
    B, L, _ = z.shape
    f32 = jnp.float32
    sh = shift_grid(z) if grid else shift_seq(z)
    zs = z + (sh - z) * p["tm_mu"]
    o1 = 3 * C_A + 2 * DECAY_LORA
    o2 = o1 + 2 * ICLR_LORA
    r, k, v, wl, al, gl = jnp.split(zs, [C_A, 2 * C_A, 3 * C_A, o1, o2], axis=-1)

    def heads(t):
        return t.astype(f32).reshape(B, L, H_A, HEAD_DIM)

    g = jax.nn.sigmoid(gl) @ p["tm_g_b"]
    kk = heads(k * p["tm_k_k"])
    kk = kk / jnp.maximum(jnp.sqrt(jnp.sum(kk * kk, -1, keepdims=True)), 1e-12)
    rh, vh = heads(r), heads(v)
    rk = p["tm_r_k"].astype(f32)
    y = 0.0
    bonus = 0.0
    states = []
    for d in range(2):
        wd = p["tm_w0"][d] + jnp.tanh(wl[..., d * DECAY_LORA:(d + 1) * DECAY_LORA]) @ p["tm_w_b"][d]
        decay = jnp.exp(-jnp.exp(-jax.nn.softplus(-wd.astype(f32)) - 0.5))
        ad = jax.nn.sigmoid(p["tm_a0"][d] + al[..., d * ICLR_LORA:(d + 1) * ICLR_LORA] @ p["tm_a_b"][d])
        kd = heads(k * (1 + (ad - 1) * p["tm_k_a"]))
        yd, Sd = wkv_scan(S0[:, d].astype(f32), rh, heads(decay), kd, vh, kk, heads(ad),
                          reverse=(d == 1))
        y = y + yd
        bonus = bonus + jnp.sum(rh * kd * rk, -1, keepdims=True) * vh
        states.append(Sd)
    mu = jnp.mean(y, -1, keepdims=True)
    var = jnp.mean(jnp.square(y - mu), -1, keepdims=True)
    yn = ((y - mu) * lax.rsqrt(var + GN_EPS)).reshape(B, L, C_A)
    yn = yn * p["tm_lnx_g"].astype(f32) + p["tm_lnx_b"].astype(f32)
    out = (yn + bonus.reshape(B, L, C_A)) * g.astype(f32)
    return out.astype(z.dtype) @ p["w_o_a"], jnp.stack(states, axis=1)


def conformer_conv(z, p):
    a, b = jnp.split(z, 2, axis=-1)
    u = a * jax.nn.sigmoid(b)
    u = dwconv(u, p["cv_w"], p["cv_b"])
    u = jax.nn.silu(layer_norm(u, p["cv_ln_g"], p["cv_ln_b"]))
    return u @ p["w_o_b"]


def trunk_layer(x, mod, S0, grid, p):
    shift1, scale1, gate1, shift2, scale2, gate2 = jnp.split(mod, 6, axis=-1)
    h = x * (1 + scale1) + shift1
    z = h @ p["w_in"]
    z_a = z[..., :C_RW]
    z_b = z[..., C_RW:C_RW + 2 * C_B]
    g_a, g_b = jnp.split(z[..., C_RW + 2 * C_B:], 2, axis=-1)
    y_a, S = rwkv_mixer(z_a, S0, grid, p)
    y_b = conformer_conv(z_b, p)
    y = (jax.nn.sigmoid(g_a) * y_a + jax.nn.sigmoid(g_b) * y_b) @ p["w_out"]
    x = layer_norm(ALPHA * x + gate1 * y, p["ln1_g"], p["ln1_b"])
    h = x * (1 + scale2) + shift2
    u = dwconv(h @ p["w_up"], p["ffn_cv_w"], p["ffn_cv_b"])
    u_g, u_v = jnp.split(u, 2, axis=-1)
    f = (jax.nn.silu(u_g) * u_v) @ p["w_down"]
    x = layer_norm(ALPHA * x + gate2 * f, p["ln2_g"], p["ln2_b"])
    return x, S


def setup_inputs(seed: int = 0) -> dict:
    key = jax.random.key(seed)
    ks = iter(jax.random.split(key, 40))

    def nrm(shape, scale):
        return jax.random.normal(next(ks), shape, jnp.float32) * scale

    D, L = D_MODEL, DEPTH
    return {
        "x_prompt": nrm((BATCH, SEQ, D), 1.0),
        "x_sample": nrm((DEC_BATCH, DEC_SEQ, D), 1.0),
        "state_rwkv": nrm((DEC_BATCH, DEPTH, 2, H_A, HEAD_DIM, HEAD_DIM), 0.3),
        "c": nrm((DEC_BATCH, D), 1.0),
        "c_ctx": nrm((D,), 1.0),
        "w_mod": nrm((L, D, 6 * D), 0.5 * D ** -0.5),
        "b_mod": nrm((L, 6 * D), 0.02),
        "w_in": nrm((L, D, C_IN), D ** -0.5),
        "tm_mu": jax.random.uniform(next(ks), (L, C_RW), jnp.float32),
        "tm_w0": jax.random.uniform(next(ks), (L, 2, C_A), jnp.float32, -5.0, 0.0),
        "tm_w_b": nrm((L, 2, DECAY_LORA, C_A), 0.3 * DECAY_LORA ** -0.5),
        "tm_a0": nrm((L, 2, C_A), 0.5),
        "tm_a_b": nrm((L, 2, ICLR_LORA, C_A), 0.5 * ICLR_LORA ** -0.5),
        "tm_g_b": nrm((L, GATE_LORA, C_A), GATE_LORA ** -0.5),
        "tm_k_k": 0.85 + nrm((L, C_A), 0.05),
        "tm_k_a": 1.0 + nrm((L, C_A), 0.05),
        "tm_r_k": nrm((L, H_A, HEAD_DIM), 0.1),
        "tm_lnx_g": 1.0 + nrm((L, C_A), 0.02),
        "tm_lnx_b": nrm((L, C_A), 0.02),
        "w_o_a": nrm((L, C_A, D), C_A ** -0.5),
        "cv_w": nrm((L, CONV_K, C_B), CONV_K ** -0.5),
        "cv_b": nrm((L, C_B), 0.02),
        "cv_ln_g": 1.0 + nrm((L, C_B), 0.02),
        "cv_ln_b": nrm((L, C_B), 0.02),
        "w_o_b": nrm((L, C_B, D), C_B ** -0.5),
        "w_out": nrm((L, D, D), BETA * D ** -0.5),
        "ln1_g": 1.0 + nrm((L, D), 0.02),
        "ln1_b": nrm((L, D), 0.02),
        "w_up": nrm((L, D, 2 * D_FF), D ** -0.5),
        "ffn_cv_w": nrm((L, FFN_CONV_K, 2 * D_FF), FFN_CONV_K ** -0.5),
        "ffn_cv_b": nrm((L, 2 * D_FF), 0.02),
        "w_down": nrm((L, D_FF, D), BETA * D_FF ** -0.5),
        "ln2_g": 1.0 + nrm((L, D), 0.02),
        "ln2_b": nrm((L, D), 0.02),
    }


def reference(x_prompt, x_sample, state_rwkv, c, c_ctx, w_mod, b_mod, w_in, tm_mu, tm_w0,
              tm_w_b, tm_a0, tm_a_b, tm_g_b, tm_k_k, tm_k_a, tm_r_k, tm_lnx_g, tm_lnx_b,
              w_o_a, cv_w, cv_b, cv_ln_g, cv_ln_b, w_o_b, w_out, ln1_g, ln1_b, w_up,
              ffn_cv_w, ffn_cv_b, w_down, ln2_g, ln2_b):
    xp, xs = x_prompt, x_sample
    s_ctx0 = jnp.zeros((xp.shape[0], 2, H_A, HEAD_DIM, HEAD_DIM), jnp.float32)
    new_states = []
    for l in range(DEPTH):
        p = {
            "w_in": w_in[l], "tm_mu": tm_mu[l], "tm_w0": tm_w0[l], "tm_w_b": tm_w_b[l],
            "tm_a0": tm_a0[l], "tm_a_b": tm_a_b[l], "tm_g_b": tm_g_b[l],
            "tm_k_k": tm_k_k[l], "tm_k_a": tm_k_a[l], "tm_r_k": tm_r_k[l],
            "tm_lnx_g": tm_lnx_g[l], "tm_lnx_b": tm_lnx_b[l], "w_o_a": w_o_a[l],
            "cv_w": cv_w[l], "cv_b": cv_b[l], "cv_ln_g": cv_ln_g[l], "cv_ln_b": cv_ln_b[l],
            "w_o_b": w_o_b[l], "w_out": w_out[l], "ln1_g": ln1_g[l], "ln1_b": ln1_b[l],
            "w_up": w_up[l], "ffn_cv_w": ffn_cv_w[l], "ffn_cv_b": ffn_cv_b[l],
            "w_down": w_down[l], "ln2_g": ln2_g[l], "ln2_b": ln2_b[l],
        }
        mod_ctx = (jax.nn.silu(c_ctx) @ w_mod[l] + b_mod[l])[None, None, :]
        mod_lat = (jax.nn.silu(c) @ w_mod[l] + b_mod[l])[:, None, :]
        xp, s_l = trunk_layer(xp, mod_ctx, s_ctx0, False, p)
        new_states.append(s_l)
        xs, _ = trunk_layer(xs, mod_lat, state_rwkv[:, l], True, p)
    new_state_rwkv = jnp.stack(new_states, axis=1)
    return (xp, xs, new_state_rwkv)
```

```python
import functools

import jax
import jax.numpy as jnp
from jax import lax
from jax.experimental import pallas as pl
from jax.experimental.pallas import tpu as pltpu

D_MODEL = 2048
DEPTH = 4
GRID_W = 64
HEAD_DIM = 64
C_A = D_MODEL // 2
H_A = C_A // HEAD_DIM
C_B = D_MODEL // 2
DECAY_LORA = 64
ICLR_LORA = 64
GATE_LORA = 160
D_FF = ((8 * D_MODEL // 3 + 127) // 128) * 128
C_RW = 3 * C_A + 2 * DECAY_LORA + 2 * ICLR_LORA + GATE_LORA
ALPHA = (2 * DEPTH) ** 0.25
LN_EPS = 1e-5
GN_EPS = HEAD_DIM * 1e-5

LANES = 128
SUBLANES = 8
HEAD_PAIRS = C_A // LANES
SCAN_T = 128
VMEM_LIMIT = 48 * 1024 * 1024

BF16 = jnp.bfloat16
F32 = jnp.float32


def _mm_kernel(a_ref, w_ref, o_ref):
    o_ref[...] = jnp.dot(a_ref[...], w_ref[...], preferred_element_type=F32)


def _pick(n, cands):
    for c in cands:
        if n % c == 0:
            return c
    return n


def matmul(a, w):
    m, k = a.shape
    _, n = w.shape
    tm = _pick(m, (1024, 512, 256, 128, 8))
    tn = _pick(n, (1024, 512, 256, 128))
    if k > 4096:
        tm, tn = min(tm, 512), min(tn, 512)
    return pl.pallas_call(
        _mm_kernel,
        grid=(n // tn, m // tm),
        in_specs=[pl.BlockSpec((tm, k), lambda j, i: (i, 0)),
                  pl.BlockSpec((k, tn), lambda j, i: (0, j))],
        out_specs=pl.BlockSpec((tm, tn), lambda j, i: (i, j)),
        out_shape=jax.ShapeDtypeStruct((m, n), F32),
        compiler_params=pltpu.CompilerParams(
            dimension_semantics=("arbitrary", "arbitrary"),
            vmem_limit_bytes=VMEM_LIMIT),
    )(a, w)


def _scan_kernel(rf_ref, vf_ref, kkf_ref, wf_ref, kf_ref, bf_ref,
                 rb_ref, vb_ref, kkb_ref, wb_ref, kb_ref, bb_ref, s0_ref, bd_ref,
                 yf_ref, yb_ref, sout_ref, s_scr, vt_scr, y_scr, *, nb):
    j = pl.program_id(1)
    nj = pl.num_programs(1)
    dirs = ((rf_ref, vf_ref, kkf_ref, wf_ref, kf_ref, bf_ref, yf_ref),
            (rb_ref, vb_ref, kkb_ref, wb_ref, kb_ref, bb_ref, yb_ref))
    chains = [(d, n, p) for d in range(2) for n in range(nb) for p in range(HEAD_PAIRS)]
    tile3 = (HEAD_DIM // SUBLANES, SUBLANES, LANES)
    lane = lax.broadcasted_iota(jnp.int32, (HEAD_DIM, LANES), 1)
    low_half = lane < HEAD_DIM
    lane_in_head = lane & (HEAD_DIM - 1)

    def cid(d, n, p):
        return (d * nb + n) * HEAD_PAIRS + p

    @pl.when(j == 0)
    def _():
        for d, n, p in chains:
            s_scr[cid(d, n, p)] = s0_ref[n, d, p]

    for d, n, p in chains:
        c = cid(d, n, p)
        vt = dirs[d][1][n, :, p * LANES:(p + 1) * LANES].T
        h0, h1 = vt[:HEAD_DIM], vt[HEAD_DIM:]
        vt_scr[c, 0] = jnp.where(low_half, h0, pltpu.roll(h1, HEAD_DIM, 1))
        vt_scr[c, 1] = jnp.where(low_half, pltpu.roll(h0, HEAD_DIM, 1), h1)
        y_scr[c] = jnp.zeros((2, HEAD_DIM, LANES), F32)

    def products(c, r_row, kk_row):
        s = s_scr[c].reshape(tile3)
        x = jnp.concatenate([(s * r_row).reshape(HEAD_DIM, LANES), (s * kk_row).reshape(HEAD_DIM, LANES)], axis=1)
        return jnp.dot(x.astype(BF16), bd_ref[...], preferred_element_type=F32)

    def step(i, carry):
        ts = (i, SCAN_T - 1 - i)
        tprev = (i - 1, SCAN_T - i)
        res = {}
        vcol = {}
        for d, n, p in chains:
            c = cid(d, n, p)
            rows = pl.ds(p, 1)
            tp = jnp.clip(tprev[d], 0, SCAN_T - 1)
            res[c] = products(c, dirs[d][0][n, tp, rows, :][None], dirs[d][2][n, ts[d], rows, :][None])
        for d, n, p in chains:
            if p % 2:
                continue
            c = cid(d, n, p)
            at_t = lane_in_head == (ts[d] & (HEAD_DIM - 1))
            half = ts[d] // HEAD_DIM
            x = jnp.concatenate([jnp.where(at_t, vt_scr[c, half], 0.0),
                                 jnp.where(at_t, vt_scr[c + 1, half], 0.0)], axis=1)
            vc = jnp.dot(x.astype(BF16), bd_ref[...], preferred_element_type=F32)
            vcol[c], vcol[c + 1] = vc[:, :LANES], vc[:, LANES:]
        for d, n, p in chains:
            c = cid(d, n, p)
            rows = pl.ds(p, 1)
            t = ts[d]
            tp = jnp.clip(tprev[d], 0, SCAN_T - 1)
            at_prev = (lane_in_head == (tp & (HEAD_DIM - 1))) & (i > 0)
            hp = tp // HEAD_DIM
            y_scr[c, hp] = jnp.where(at_prev, res[c][:, :LANES], y_scr[c, hp])
            skk = res[c][:, LANES:].reshape(tile3)
            w_t = dirs[d][3][0, n, t, rows, :][None]
            k_t = dirs[d][4][0, n, t, rows, :][None]
            b_t = dirs[d][5][0, n, t, rows, :][None]
            s = s_scr[c].reshape(tile3)
            s = s * w_t - skk * b_t + vcol[c].reshape(tile3) * k_t
            s_scr[c] = s.reshape(HEAD_DIM, LANES)
        return carry

    lax.fori_loop(0, SCAN_T, step, 0)

    tlast = (SCAN_T - 1, 0)
    for d, n, p in chains:
        c = cid(d, n, p)
        rows = pl.ds(p, 1)
        r_row = dirs[d][0][n, tlast[d], rows, :][None]
        yl = products(c, r_row, r_row)[:, :LANES]
        hl = tlast[d] // HEAD_DIM
        at_last = lane_in_head == (tlast[d] & (HEAD_DIM - 1))
        lo = y_scr[c, 0]
        hi = y_scr[c, 1]
        if hl == 0:
            lo = jnp.where(at_last, yl, lo)
        else:
            hi = jnp.where(at_last, yl, hi)
        y0 = jnp.where(low_half, lo, pltpu.roll(hi, HEAD_DIM, 1))
        y1 = jnp.where(low_half, pltpu.roll(lo, HEAD_DIM, 1), hi)
        dirs[d][6][n, :, p * LANES:(p + 1) * LANES] = jnp.concatenate([y0, y1], axis=0).T

    @pl.when(j == nj - 1)
    def _():
        for d, n, p in chains:
            sout_ref[n, d, p] = s_scr[cid(d, n, p)]


def _scan_const():
    row = jnp.arange(2 * LANES)[:, None]
    col = jnp.arange(2 * LANES)[None, :]
    return (row // HEAD_DIM == col // HEAD_DIM).astype(BF16)


def wkv_scan(r, v, kk, w2, k2, b2, s0):
    bsz, seq, _ = r.shape
    nb = 2
    nj = seq // SCAN_T

    def flat(rev):
        return pl.BlockSpec((nb, SCAN_T, C_A), lambda b, j: (b, nj - 1 - j if rev else j, 0))

    def rowsp(rev):
        return pl.BlockSpec((nb, SCAN_T, HEAD_PAIRS, LANES), lambda b, j: (b, nj - 1 - j if rev else j, 0, 0))

    def rows_dir(d):
        return pl.BlockSpec((1, nb, SCAN_T, HEAD_PAIRS, LANES),
                            lambda b, j: (d, b, nj - 1 - j if d else j, 0, 0))

    state = pl.BlockSpec((nb, 2, HEAD_PAIRS, HEAD_DIM, LANES), lambda b, j: (b, 0, 0, 0, 0))

    def rows4(x):
        return x.reshape(x.shape[:-1] + (HEAD_PAIRS, LANES))

    r4, kk4, w4, k4, b4 = rows4(r), rows4(kk), rows4(w2), rows4(k2), rows4(b2)
    nchains = 2 * nb * HEAD_PAIRS
    return pl.pallas_call(
        functools.partial(_scan_kernel, nb=nb),
        grid=(bsz // nb, nj),
        in_specs=[rowsp(0), flat(0), rowsp(0), rows_dir(0), rows_dir(0), rows_dir(0),
                  rowsp(1), flat(1), rowsp(1), rows_dir(1), rows_dir(1), rows_dir(1),
                  state, pl.BlockSpec((2 * LANES, 2 * LANES), lambda b, j: (0, 0))],
        out_specs=[flat(0), flat(1), state],
        out_shape=[jax.ShapeDtypeStruct((bsz, seq, C_A), F32),
                   jax.ShapeDtypeStruct((bsz, seq, C_A), F32),
                   jax.ShapeDtypeStruct((bsz, 2, HEAD_PAIRS, HEAD_DIM, LANES), F32)],
        scratch_shapes=[pltpu.VMEM((nchains, HEAD_DIM, LANES), F32),
                        pltpu.VMEM((nchains, 2, HEAD_DIM, LANES), F32),
                        pltpu.VMEM((nchains, 2, HEAD_DIM, LANES), F32)],
        compiler_params=pltpu.CompilerParams(
            dimension_semantics=("arbitrary", "arbitrary"),
            vmem_limit_bytes=VMEM_LIMIT),
    )(r4, v, kk4, w4, k4, b4, r4, v, kk4, w4, k4, b4, s0, _scan_const())


def _state_to_tiles(s):
    b = s.shape[0]
    s = s.reshape(b, 2, HEAD_PAIRS, 2, HEAD_DIM, HEAD_DIM)
    return jnp.transpose(s, (0, 1, 2, 4, 3, 5)).reshape(b, 2, HEAD_PAIRS, HEAD_DIM, LANES)


def _tiles_to_state(s):
    b = s.shape[0]
    s = s.reshape(b, 2, HEAD_PAIRS, HEAD_DIM, 2, HEAD_DIM)
    return jnp.transpose(s, (0, 1, 2, 4, 3, 5)).reshape(b, 2, H_A, HEAD_DIM, HEAD_DIM)


def _layer_norm(x, g, b):
    mu = jnp.mean(x, -1, keepdims=True)
    var = jnp.mean(jnp.square(x - mu), -1, keepdims=True)
    return (x - mu) * lax.rsqrt(var + LN_EPS) * g + b


def _dwconv(h, w, b):
    k = w.shape[0]
    half = (k - 1) // 2
    length = h.shape[1]
    hp = jnp.pad(h, ((0, 0), (half, half), (0, 0)))
    out = b
    for i in range(k):
        out = out + hp[:, i:i + length] * w[i]
    return out


def _shift_seq(h):
    prev = jnp.pad(h[:, :-1], ((0, 0), (1, 0), (0, 0)))
    nxt = jnp.pad(h[:, 1:], ((0, 0), (0, 1), (0, 0)))
    even = (jnp.arange(h.shape[-1]) % 2) == 0
    return jnp.where(even, prev, nxt)


def _shift_grid(h):
    b, l, c = h.shape
    rows = l // GRID_W
    g = h.reshape(b, rows, GRID_W, c)
    left = jnp.pad(g[:, :, :-1], ((0, 0), (0, 0), (1, 0), (0, 0)))
    right = jnp.pad(g[:, :, 1:], ((0, 0), (0, 0), (0, 1), (0, 0)))
    up = jnp.pad(g[:, :-1], ((0, 0), (1, 0), (0, 0), (0, 0)))
    down = jnp.pad(g[:, 1:], ((0, 0), (0, 1), (0, 0), (0, 0)))
    q = jnp.arange(c) % 4
    out = jnp.where(q == 0, left, jnp.where(q == 1, right, jnp.where(q == 2, up, down)))
    return out.reshape(b, l, c)


def _pad_rows(w, rows):
    return jnp.pad(w, ((0, rows - w.shape[0]), (0, 0)))


def _pad_cols(x, cols):
    return jnp.pad(x, ((0, 0), (0, cols - x.shape[1])))


def _rwkv_pre(z_a, grid, p):
    sh = _shift_grid(z_a) if grid else _shift_seq(z_a)
    zs = z_a + (sh - z_a) * p["tm_mu"]
    o1 = 3 * C_A + 2 * DECAY_LORA
    o2 = o1 + 2 * ICLR_LORA
    r, k, v, wl, al, gl = jnp.split(zs, [C_A, 2 * C_A, 3 * C_A, o1, o2], axis=-1)
    return r, k, v, wl, al, gl


def _layer(xp, xs, mod, s_lat, p):
    bp, lp, _ = xp.shape
    bs, ls, _ = xs.shape
    tp = bp * lp
    mod_p = mod[0][None, None, :]
    mod_s = mod[1:3][:, None, :]
    sh1p, sc1p, g1p, sh2p, sc2p, g2p = jnp.split(mod_p, 6, axis=-1)
    sh1s, sc1s, g1s, sh2s, sc2s, g2s = jnp.split(mod_s, 6, axis=-1)

    def both(fp, fs):
        return jnp.concatenate([fp.reshape(tp, -1), fs.reshape(bs * ls, -1)], axis=0)

    def split(t):
        return t[:tp].reshape(bp, lp, -1), t[tp:].reshape(bs, ls, -1)

    h = both(xp * (1 + sc1p) + sh1p, xs * (1 + sc1s) + sh1s)
    z = matmul(h.astype(BF16), p["w_in"])
    zp, zs_ = split(z)

    pre = []
    for zz, grid in ((zp, False), (zs_, True)):
        pre.append(_rwkv_pre(zz[..., :C_RW], grid, p))
    r = both(pre[0][0], pre[1][0])
    k = both(pre[0][1], pre[1][1])
    v = both(pre[0][2], pre[1][2])
    wl = both(pre[0][3], pre[1][3])
    al = both(pre[0][4], pre[1][4])
    gl = both(pre[0][5], pre[1][5])

    g = matmul(_pad_cols(jax.nn.sigmoid(gl), 256).astype(BF16), p["tm_g_b"])
    wd = matmul(jnp.tanh(wl).astype(BF16), p["tm_w_b"]) + p["tm_w0"]
    decay = jnp.exp(-jnp.exp(-jax.nn.softplus(-wd) - 0.5))
    ad = jax.nn.sigmoid(matmul(al.astype(BF16), p["tm_a_b"]) + p["tm_a0"])
    kk = (k * p["tm_k_k"]).reshape(-1, H_A, HEAD_DIM)
    kk = kk / jnp.maximum(jnp.sqrt(jnp.sum(kk * kk, -1, keepdims=True)), 1e-12)
    kk = kk.reshape(-1, C_A)
    ttot = r.shape[0]
    ad2 = jnp.transpose(ad.reshape(ttot, 2, C_A), (1, 0, 2))
    w2 = jnp.transpose(decay.reshape(ttot, 2, C_A), (1, 0, 2))
    k2 = k[None] * (1 + (ad2 - 1) * p["tm_k_a"])
    b2 = kk[None] * ad2

    ys = []
    states = []
    for lo, hi, bb, ll, s0 in ((0, tp, bp, lp, jnp.zeros((bp, 2, HEAD_PAIRS, HEAD_DIM, LANES), F32)),
                               (tp, ttot, bs, ls, _state_to_tiles(s_lat))):
        yf, yb, s_out = wkv_scan(r[lo:hi].reshape(bb, ll, C_A), v[lo:hi].reshape(bb, ll, C_A),
                                 kk[lo:hi].reshape(bb, ll, C_A),
                                 w2[:, lo:hi].reshape(2, bb, ll, C_A), k2[:, lo:hi].reshape(2, bb, ll, C_A),
                                 b2[:, lo:hi].reshape(2, bb, ll, C_A), s0)
        ys.append((yf + yb).reshape(bb * ll, C_A))
        states.append(s_out)
    y = jnp.concatenate(ys, axis=0).reshape(ttot, H_A, HEAD_DIM)
    mu = jnp.mean(y, -1, keepdims=True)
    var = jnp.mean(jnp.square(y - mu), -1, keepdims=True)
    yn = ((y - mu) * lax.rsqrt(var + GN_EPS)).reshape(ttot, C_A)
    yn = yn * p["tm_lnx_g"] + p["tm_lnx_b"]
    rk = p["tm_r_k"].reshape(C_A)
    bonus = jnp.sum((r * (k2[0] + k2[1]) * rk).reshape(ttot, H_A, HEAD_DIM), -1, keepdims=True)
    bonus = (bonus * v.reshape(ttot, H_A, HEAD_DIM)).reshape(ttot, C_A)
    y_a = matmul(((yn + bonus) * g).astype(BF16), p["w_o_a"])

    ub = []
    for zz in (zp, zs_):
        a_, b_ = zz[..., C_RW:C_RW + C_B], zz[..., C_RW + C_B:C_RW + 2 * C_B]
        u = _dwconv(a_ * jax.nn.sigmoid(b_), p["cv_w"], p["cv_b"])
        ub.append(jax.nn.silu(_layer_norm(u, p["cv_ln_g"], p["cv_ln_b"])))
    u = both(ub[0], ub[1])
    y_b = matmul(u.astype(BF16), p["w_o_b"])

    g_a = z[:, C_RW + 2 * C_B:C_RW + 2 * C_B + D_MODEL]
    g_b = z[:, C_RW + 2 * C_B + D_MODEL:C_RW + 2 * C_B + 2 * D_MODEL]
    ymix = jax.nn.sigmoid(g_a) * y_a + jax.nn.sigmoid(g_b) * y_b
    yo = matmul(ymix.astype(BF16), p["w_out"])
    yop, yos = split(yo)
    xp = _layer_norm(ALPHA * xp + g1p * yop, p["ln1_g"], p["ln1_b"])
    xs = _layer_norm(ALPHA * xs + g1s * yos, p["ln1_g"], p["ln1_b"])

    h2 = both(xp * (1 + sc2p) + sh2p, xs * (1 + sc2s) + sh2s)
    up = matmul(h2.astype(BF16), p["w_up"])
    upp, ups = split(up)
    fs = []
    for uu in (upp, ups):
        uu = _dwconv(uu[..., :2 * D_FF], p["ffn_cv_w"], p["ffn_cv_b"])
        fs.append(jax.nn.silu(uu[..., :D_FF]) * uu[..., D_FF:2 * D_FF])
    f = matmul(both(fs[0], fs[1]).astype(BF16), p["w_down"])
    fp, fs_ = split(f)
    xp = _layer_norm(ALPHA * xp + g2p * fp, p["ln2_g"], p["ln2_b"])
    xs = _layer_norm(ALPHA * xs + g2s * fs_, p["ln2_g"], p["ln2_b"])
    return xp, xs, _tiles_to_state(states[0])


def _blockdiag2(w):
    z = jnp.zeros_like(w[0])
    return jnp.concatenate([jnp.concatenate([w[0], z], axis=1), jnp.concatenate([z, w[1]], axis=1)], axis=0)


def kernel(x_prompt, x_sample, state_rwkv, c, c_ctx, w_mod, b_mod, w_in, tm_mu, tm_w0, tm_w_b, tm_a0,
           tm_a_b, tm_g_b, tm_k_k, tm_k_a, tm_r_k, tm_lnx_g, tm_lnx_b, w_o_a, cv_w, cv_b, cv_ln_g,
           cv_ln_b, w_o_b, w_out, ln1_g, ln1_b, w_up, ffn_cv_w, ffn_cv_b, w_down, ln2_g, ln2_b):
    xp, xs = x_prompt, x_sample
    cond = jnp.concatenate([c_ctx[None], c, jnp.zeros((5, D_MODEL), F32)], axis=0)
    cond = jax.nn.silu(cond).astype(BF16)
    new_states = []
    for l in range(DEPTH):
        mod = matmul(cond, w_mod[l].astype(BF16))[:3] + b_mod[l]
        p = {
            "w_in": _pad_cols(w_in[l], 9728).astype(BF16),
            "tm_mu": tm_mu[l],
            "tm_w0": tm_w0[l].reshape(2 * C_A),
            "tm_w_b": _blockdiag2(tm_w_b[l]).astype(BF16),
            "tm_a0": tm_a0[l].reshape(2 * C_A),
            "tm_a_b": _blockdiag2(tm_a_b[l]).astype(BF16),
            "tm_g_b": _pad_rows(tm_g_b[l], 256).astype(BF16),
            "tm_k_k": tm_k_k[l], "tm_k_a": tm_k_a[l], "tm_r_k": tm_r_k[l],
            "tm_lnx_g": tm_lnx_g[l], "tm_lnx_b": tm_lnx_b[l],
            "w_o_a": w_o_a[l].astype(BF16),
            "cv_w": cv_w[l], "cv_b": cv_b[l], "cv_ln_g": cv_ln_g[l], "cv_ln_b": cv_ln_b[l],
            "w_o_b": w_o_b[l].astype(BF16), "w_out": w_out[l].astype(BF16),
            "ln1_g": ln1_g[l], "ln1_b": ln1_b[l],
            "w_up": _pad_cols(w_up[l], 11264).astype(BF16),
            "ffn_cv_w": ffn_cv_w[l], "ffn_cv_b": ffn_cv_b[l],
            "w_down": w_down[l].astype(BF16),
            "ln2_g": ln2_g[l], "ln2_b": ln2_b[l],
        }
        xp, xs, s_l = _layer(xp, xs, mod, state_rwkv[:, l], p)
        new_states.append(s_l)
    return (xp, xs, jnp.stack(new_states, axis=1))
```

```python
import functools
import math

import jax
import jax.numpy as jnp
from jax import lax
from jax.experimental import pallas as pl
from jax.experimental.pallas import tpu as pltpu

D_MODEL = 2048
DEPTH = 4
GRID_W = 64
HEAD_DIM = 64
C_A = D_MODEL // 2
H_A = C_A // HEAD_DIM
C_B = D_MODEL // 2
DECAY_LORA = 64
ICLR_LORA = 64
GATE_LORA = 160
CONV_K = 31
FFN_CONV_K = 3
D_FF = ((8 * D_MODEL // 3 + 127) // 128) * 128
C_RW = 3 * C_A + 2 * DECAY_LORA + 2 * ICLR_LORA + GATE_LORA
ALPHA = (2 * DEPTH) ** 0.25
LN_EPS = 1e-5
GN_EPS = HEAD_DIM * 1e-5

LANES = 128
SUBLANES = 8
HEAD_PAIRS = C_A // LANES
SCAN_T = 128
VMEM_LIMIT = 56 * 1024 * 1024
ROW_BLOCK = 2048
DECAY_SCALE = math.exp(-0.5)
MOD_ROWS = 16

Z_R, Z_K, Z_V, Z_A = 0, C_A, 2 * C_A, 3 * C_A
Z_GA = Z_A + C_B
Z_GB = Z_GA + D_MODEL
Z_B = Z_GB + D_MODEL
Z_LORA = Z_B + C_B
LORA_W = 4 * LANES
C_Z = Z_LORA + LORA_W
FF_PAD = ((D_FF + 4 * LANES - 1) // (4 * LANES)) * (4 * LANES)

BF16 = jnp.bfloat16
F32 = jnp.float32


def _scan_kernel(rf_ref, vf_ref, kkf_ref, wf_ref, kf_ref, bf_ref,
                 rb_ref, vb_ref, kkb_ref, wb_ref, kb_ref, bb_ref, s0_ref, bd_ref,
                 yf_ref, yb_ref, sout_ref, s_scr, vt_scr, y_scr, rows_scr, *, nb):
    j = pl.program_id(1)
    nj = pl.num_programs(1)
    dirs = ((rf_ref, vf_ref, kkf_ref, wf_ref, kf_ref, bf_ref, yf_ref),
            (rb_ref, vb_ref, kkb_ref, wb_ref, kb_ref, bb_ref, yb_ref))
    chains = [(d, n, p) for d in range(2) for n in range(nb) for p in range(HEAD_PAIRS)]
    tile3 = (HEAD_DIM // SUBLANES, SUBLANES, LANES)
    R_, KK_, W_, K_, B_ = range(5)
    lane = lax.broadcasted_iota(jnp.int32, (HEAD_DIM, LANES), 1)
    low_half = lane < HEAD_DIM
    lane_in_head = lane & (HEAD_DIM - 1)

    def cid(d, n, p):
        return (d * nb + n) * HEAD_PAIRS + p

    @pl.when(j == 0)
    def _():
        for d, n, p in chains:
            s_scr[cid(d, n, p)] = s0_ref[n, d, p]

    for d, n, p in chains:
        c = cid(d, n, p)
        vt = dirs[d][1][n, :, p * LANES:(p + 1) * LANES].T
        h0, h1 = vt[:HEAD_DIM], vt[HEAD_DIM:]
        vt_scr[c, 0] = jnp.where(low_half, h0, pltpu.roll(h1, HEAD_DIM, 1))
        vt_scr[c, 1] = jnp.where(low_half, pltpu.roll(h0, HEAD_DIM, 1), h1)
        y_scr[c] = jnp.zeros((2, HEAD_DIM, LANES), F32)

    for d in range(2):
        for k, src in enumerate((dirs[d][0], dirs[d][2], dirs[d][3], dirs[d][4], dirs[d][5])):
            for n in range(nb):
                blk = src[n] if k < 2 else src[0, n]
                rows_scr[d * 5 + k, n] = blk.reshape(SCAN_T, HEAD_PAIRS, LANES)

    def row(d, k, n, t, p):
        return rows_scr[d * 5 + k, n, t, pl.ds(p, 1), :][None]

    def weighted(c, r_row, kk_row):
        s = s_scr[c].reshape(tile3)
        return jnp.concatenate(
            [(s * r_row).reshape(HEAD_DIM, LANES), (s * kk_row).reshape(HEAD_DIM, LANES)], axis=1).astype(BF16)

    def step(i, carry):
        ts = (i, SCAN_T - 1 - i)
        tprev = (i - 1, SCAN_T - i)
        lhs = []
        for d, n, p in chains:
            c = cid(d, n, p)
            tp = jnp.clip(tprev[d], 0, SCAN_T - 1)
            lhs.append(weighted(c, row(d, R_, n, tp, p), row(d, KK_, n, ts[d], p)))
        for d, n, p in chains:
            if p % 2:
                continue
            c = cid(d, n, p)
            at_t = lane_in_head == (ts[d] & (HEAD_DIM - 1))
            half = ts[d] // HEAD_DIM
            lhs.append(jnp.concatenate([jnp.where(at_t, vt_scr[c, half], 0.0),
                                        jnp.where(at_t, vt_scr[c + 1, half], 0.0)], axis=1).astype(BF16))
        res = jnp.dot(jnp.concatenate(lhs, axis=0), bd_ref[...], preferred_element_type=F32)
        nch = len(chains)
        for d, n, p in chains:
            c = cid(d, n, p)
            t = ts[d]
            tp = jnp.clip(tprev[d], 0, SCAN_T - 1)
            at_prev = (lane_in_head == (tp & (HEAD_DIM - 1))) & (i > 0)
            hp = tp // HEAD_DIM
            rc = res[c * HEAD_DIM:(c + 1) * HEAD_DIM]
            y_scr[c, hp] = jnp.where(at_prev, rc[:, :LANES], y_scr[c, hp])
            skk = rc[:, LANES:].reshape(tile3)
            vrow = (nch + c // 2) * HEAD_DIM
            vcol = res[vrow:vrow + HEAD_DIM, (c % 2) * LANES:(c % 2 + 1) * LANES].reshape(tile3)
            s = s_scr[c].reshape(tile3)
            s = s * row(d, W_, n, t, p) - skk * row(d, B_, n, t, p) + vcol * row(d, K_, n, t, p)
            s_scr[c] = s.reshape(HEAD_DIM, LANES)
        return carry

    lax.fori_loop(0, SCAN_T, step, 0)

    tlast = (SCAN_T - 1, 0)
    for d, n, p in chains:
        c = cid(d, n, p)
        r_row = row(d, R_, n, tlast[d], p)
        yl = jnp.dot(weighted(c, r_row, r_row), bd_ref[...], preferred_element_type=F32)[:, :LANES]
        hl = tlast[d] // HEAD_DIM
        at_last = lane_in_head == (tlast[d] & (HEAD_DIM - 1))
        lo = y_scr[c, 0]
        hi = y_scr[c, 1]
        if hl == 0:
            lo = jnp.where(at_last, yl, lo)
        else:
            hi = jnp.where(at_last, yl, hi)
        y0 = jnp.where(low_half, lo, pltpu.roll(hi, HEAD_DIM, 1))
        y1 = jnp.where(low_half, pltpu.roll(lo, HEAD_DIM, 1), hi)
        dirs[d][6][n, :, p * LANES:(p + 1) * LANES] = jnp.concatenate([y0, y1], axis=0).T

    @pl.when(j == nj - 1)
    def _():
        for d, n, p in chains:
            sout_ref[n, d, p] = s_scr[cid(d, n, p)]


def _head_ones(n):
    i = jnp.arange(n)
    return (i[:, None] // HEAD_DIM == i[None, :] // HEAD_DIM).astype(BF16)


def wkv_scan(r, v, kk, w2, k2, b2, s0):
    bsz, seq, _ = r.shape
    nb = 2
    nj = seq // SCAN_T

    def flat(rev):
        return pl.BlockSpec((nb, SCAN_T, C_A), lambda b, j: (b, nj - 1 - j if rev else j, 0))

    def flat_dir(d):
        return pl.BlockSpec((1, nb, SCAN_T, C_A), lambda b, j: (d, b, nj - 1 - j if d else j, 0))

    state = pl.BlockSpec((nb, 2, HEAD_PAIRS, HEAD_DIM, LANES), lambda b, j: (b, 0, 0, 0, 0))

    nchains = 2 * nb * HEAD_PAIRS
    return pl.pallas_call(
        functools.partial(_scan_kernel, nb=nb),
        grid=(bsz // nb, nj),
        in_specs=[flat(0), flat(0), flat(0), flat_dir(0), flat_dir(0), flat_dir(0),
                  flat(1), flat(1), flat(1), flat_dir(1), flat_dir(1), flat_dir(1),
                  state, pl.BlockSpec((2 * LANES, 2 * LANES), lambda b, j: (0, 0))],
        out_specs=[flat(0), flat(1), state],
        out_shape=[jax.ShapeDtypeStruct((bsz, seq, C_A), F32),
                   jax.ShapeDtypeStruct((bsz, seq, C_A), F32),
                   jax.ShapeDtypeStruct((bsz, 2, HEAD_PAIRS, HEAD_DIM, LANES), F32)],
        scratch_shapes=[pltpu.VMEM((nchains, HEAD_DIM, LANES), F32),
                        pltpu.VMEM((nchains, 2, HEAD_DIM, LANES), F32),
                        pltpu.VMEM((nchains, 2, HEAD_DIM, LANES), F32),
                        pltpu.VMEM((10, nb, SCAN_T, HEAD_PAIRS, LANES), F32)],
        compiler_params=pltpu.CompilerParams(
            dimension_semantics=("arbitrary", "arbitrary"),
            vmem_limit_bytes=VMEM_LIMIT),
    )(r, v, kk, w2, k2, b2, r, v, kk, w2, k2, b2, s0, _head_ones(2 * LANES))


def _state_to_tiles(s):
    b = s.shape[0]
    s = s.reshape(b, 2, HEAD_PAIRS, 2, HEAD_DIM, HEAD_DIM)
    return jnp.transpose(s, (0, 1, 2, 4, 3, 5)).reshape(b, 2, HEAD_PAIRS, HEAD_DIM, LANES)


def _tiles_to_state(s):
    b = s.shape[0]
    s = s.reshape(b, 2, HEAD_PAIRS, HEAD_DIM, 2, HEAD_DIM)
    return jnp.transpose(s, (0, 1, 2, 4, 3, 5)).reshape(b, 2, H_A, HEAD_DIM, HEAD_DIM)


def _rows_in_seq(shape, seq_len):
    return lax.broadcasted_iota(jnp.int32, shape, 0) & (seq_len - 1)


def _shift_rows(x, offset, rs, seq_len):
    rolled = pltpu.roll(x, (-offset) % x.shape[0], 0)
    ok = (rs >= -offset) if offset < 0 else (rs < seq_len - offset)
    return jnp.where(ok, rolled, 0.0)


def _token_shift(x, seq_len, grid):
    rs = _rows_in_seq(x.shape, seq_len)
    col = lax.broadcasted_iota(jnp.int32, x.shape, 1)
    if not grid:
        return jnp.where((col & 1) == 0, _shift_rows(x, -1, rs, seq_len), _shift_rows(x, 1, rs, seq_len))
    gc = rs & (GRID_W - 1)
    left = jnp.where(gc == 0, 0.0, pltpu.roll(x, 1, 0))
    right = jnp.where(gc == GRID_W - 1, 0.0, pltpu.roll(x, x.shape[0] - 1, 0))
    up = _shift_rows(x, -GRID_W, rs, seq_len)
    down = _shift_rows(x, GRID_W, rs, seq_len)
    q = col & 3
    return jnp.where(q == 0, left, jnp.where(q == 1, right, jnp.where(q == 2, up, down)))


def _split_dot(x, ones):
    hi = x.astype(BF16)
    lo = (x - hi.astype(F32)).astype(BF16)
    return (jnp.dot(hi, ones, preferred_element_type=F32) + jnp.dot(lo, ones, preferred_element_type=F32))


def _rwkv_pre_kernel(zr_ref, zk_ref, zv_ref, zl_ref, mur_ref, muk_ref, muv_ref, mul_ref,
                     w0_ref, a0_ref, kkw_ref, kaw_ref, rkw_ref, ww_ref, wa_ref, wg_ref, ones_ref,
                     r_ref, v_ref, kk_ref, g_ref, bonus_ref, w2_ref, k2_ref, b2_ref,
                     twl_scr, al_scr, sgl_scr, *, seq_len, grid):
    def mix(z, mu):
        return z + (_token_shift(z, seq_len, grid) - z) * mu

    @pl.when(pl.program_id(1) == 0)
    def _():
        zl = mix(zl_ref[...], mul_ref[...])
        twl_scr[...] = jnp.tanh(zl[:, :LANES]).astype(BF16)
        al_scr[...] = zl[:, LANES:2 * LANES].astype(BF16)
        sgl_scr[...] = jax.nn.sigmoid(zl[:, 2 * LANES:]).astype(BF16)

    r = mix(zr_ref[...], mur_ref[...])
    k = mix(zk_ref[...], muk_ref[...])
    v = mix(zv_ref[...], muv_ref[...])
    ones = ones_ref[...]
    kx = k * kkw_ref[...]
    kk = kx / jnp.maximum(jnp.sqrt(_split_dot(kx * kx, ones)), 1e-12)
    wd = jnp.dot(twl_scr[...], ww_ref[0], preferred_element_type=F32)
    ad = jnp.dot(al_scr[...], wa_ref[0], preferred_element_type=F32)
    ksum = None
    for d in range(2):
        cols = slice(d * LANES, (d + 1) * LANES)
        w2_ref[d] = jnp.exp(-DECAY_SCALE * jax.nn.sigmoid(wd[:, cols] + w0_ref[d:d + 1, :]))
        a_d = jax.nn.sigmoid(ad[:, cols] + a0_ref[d:d + 1, :])
        k_d = k * (1.0 + (a_d - 1.0) * kaw_ref[...])
        k2_ref[d] = k_d
        b2_ref[d] = kk * a_d
        ksum = k_d if ksum is None else ksum + k_d
    r_ref[...] = r
    v_ref[...] = v
    kk_ref[...] = kk
    g_ref[...] = jnp.dot(sgl_scr[...], wg_ref[0], preferred_element_type=F32)
    bonus_ref[...] = _split_dot(r * ksum * rkw_ref[...], ones) * v


def rwkv_pre(z, mu, p, seq_len, grid):
    t = z.shape[0]
    rb = ROW_BLOCK

    def zcol(off):
        return pl.BlockSpec((rb, LANES), lambda i, q: (i, off + q))

    def vcol(off):
        return pl.BlockSpec((1, LANES), lambda i, q: (0, off + q))

    def pair_vec(rows):
        return pl.BlockSpec((rows, LANES), lambda i, q: (0, q))

    def pair_w(a, b):
        return pl.BlockSpec((1, a, b), lambda i, q: (q, 0, 0))

    lora_blk = Z_LORA // LORA_W
    out1 = pl.BlockSpec((rb, LANES), lambda i, q: (i, q))
    out2 = pl.BlockSpec((2, rb, LANES), lambda i, q: (0, i, q))
    s1 = jax.ShapeDtypeStruct((t, C_A), F32)
    s2 = jax.ShapeDtypeStruct((2, t, C_A), F32)
    return pl.pallas_call(
        functools.partial(_rwkv_pre_kernel, seq_len=seq_len, grid=grid),
        grid=(t // rb, HEAD_PAIRS),
        in_specs=[zcol(Z_R // LANES), zcol(Z_K // LANES), zcol(Z_V // LANES),
                  pl.BlockSpec((rb, LORA_W), lambda i, q: (i, lora_blk)),
                  vcol(Z_R // LANES), vcol(Z_K // LANES), vcol(Z_V // LANES),
                  pl.BlockSpec((1, LORA_W), lambda i, q: (0, lora_blk)),
                  pair_vec(2), pair_vec(2), pair_vec(1), pair_vec(1), pair_vec(1),
                  pair_w(LANES, 2 * LANES), pair_w(LANES, 2 * LANES), pair_w(2 * LANES, LANES),
                  pl.BlockSpec((LANES, LANES), lambda i, q: (0, 0))],
        out_specs=[out1, out1, out1, out1, out1, out2, out2, out2],
        out_shape=[s1, s1, s1, s1, s1, s2, s2, s2],
        scratch_shapes=[pltpu.VMEM((rb, LANES), BF16), pltpu.VMEM((rb, LANES), BF16),
                        pltpu.VMEM((rb, 2 * LANES), BF16)],
        compiler_params=pltpu.CompilerParams(
            dimension_semantics=("arbitrary", "arbitrary"), vmem_limit_bytes=VMEM_LIMIT),
    )(z, z, z, z, mu, mu, mu, mu, p["tm_w0"], p["tm_a0"], p["tm_k_k"], p["tm_k_a"], p["tm_r_k"],
      p["tm_w_b"], p["tm_a_b"], p["tm_g_b"], _head_ones(LANES))


def _glu_conv_kernel(a_ref, b_ref, w_ref, cb_ref, o_ref, *, seq_len):
    u = a_ref[...] * jax.nn.sigmoid(b_ref[...])
    rs = _rows_in_seq(u.shape, seq_len)
    half = (CONV_K - 1) // 2
    acc = cb_ref[...] + u * w_ref[half:half + 1, :]
    for k in range(CONV_K):
        if k != half:
            acc = acc + _shift_rows(u, k - half, rs, seq_len) * w_ref[k:k + 1, :]
    o_ref[...] = acc


def glu_conv(z, cv_w, cv_b, seq_len):
    t = z.shape[0]
    rb, cw = ROW_BLOCK, 2 * LANES
    return pl.pallas_call(
        functools.partial(_glu_conv_kernel, seq_len=seq_len),
        grid=(t // rb, C_B // cw),
        in_specs=[pl.BlockSpec((rb, cw), lambda i, j: (i, Z_A // cw + j)),
                  pl.BlockSpec((rb, cw), lambda i, j: (i, Z_B // cw + j)),
                  pl.BlockSpec((CONV_K, cw), lambda i, j: (0, j)),
                  pl.BlockSpec((1, cw), lambda i, j: (0, j))],
        out_specs=pl.BlockSpec((rb, cw), lambda i, j: (i, j)),
        out_shape=jax.ShapeDtypeStruct((t, C_B), F32),
        compiler_params=pltpu.CompilerParams(
            dimension_semantics=("arbitrary", "arbitrary"), vmem_limit_bytes=VMEM_LIMIT),
    )(z, z, cv_w, cv_b)


def _ffn_act_kernel(g_ref, v_ref, wg_ref, wv_ref, bg_ref, bv_ref, o_ref, *, seq_len):
    rs = _rows_in_seq(g_ref.shape, seq_len)

    def conv3(x, w_ref, b_ref):
        return (b_ref[...] + _shift_rows(x, -1, rs, seq_len) * w_ref[0:1, :] + x * w_ref[1:2, :]
                + _shift_rows(x, 1, rs, seq_len) * w_ref[2:3, :])

    ug = conv3(g_ref[...], wg_ref, bg_ref)
    uv = conv3(v_ref[...], wv_ref, bv_ref)
    o_ref[...] = (ug * jax.nn.sigmoid(ug) * uv).astype(BF16)


def ffn_act(up, cw_, cb_, seq_len):
    t = up.shape[0]
    rb, cw = ROW_BLOCK, 4 * LANES
    nj = FF_PAD // cw
    return pl.pallas_call(
        functools.partial(_ffn_act_kernel, seq_len=seq_len),
        grid=(t // rb, nj),
        in_specs=[pl.BlockSpec((rb, cw), lambda i, j: (i, j)),
                  pl.BlockSpec((rb, cw), lambda i, j: (i, nj + j)),
                  pl.BlockSpec((FFN_CONV_K, cw), lambda i, j: (0, j)),
                  pl.BlockSpec((FFN_CONV_K, cw), lambda i, j: (0, nj + j)),
                  pl.BlockSpec((1, cw), lambda i, j: (0, j)),
                  pl.BlockSpec((1, cw), lambda i, j: (0, nj + j))],
        out_specs=pl.BlockSpec((rb, cw), lambda i, j: (i, j)),
        out_shape=jax.ShapeDtypeStruct((t, FF_PAD), BF16),
        compiler_params=pltpu.CompilerParams(
            dimension_semantics=("arbitrary", "arbitrary"), vmem_limit_bytes=VMEM_LIMIT),
    )(up, up, cw_, cw_, cb_, cb_)


def _seq_of_block(tm, seq_len, nseq):
    if nseq == 1:
        return lambda i: 0
    return lambda i: (i * tm) // seq_len


def _mod_mm_kernel(x_ref, mod_ref, w_ref, o_ref, h_scr, *, shift_row, scale_row):
    @pl.when(pl.program_id(1) == 0)
    def _():
        h = x_ref[...] * (1.0 + mod_ref[0, scale_row:scale_row + 1, :]) + mod_ref[0, shift_row:shift_row + 1, :]
        h_scr[...] = h.astype(BF16)

    o_ref[...] = jnp.dot(h_scr[...], w_ref[...], preferred_element_type=F32)


def mod_matmul(x, mod, w, shift_row, scale_row, seq_len):
    t, d = x.shape
    n = w.shape[1]
    tm, tn = 512, 512
    seq = _seq_of_block(tm, seq_len, mod.shape[0])
    return pl.pallas_call(
        functools.partial(_mod_mm_kernel, shift_row=shift_row, scale_row=scale_row),
        grid=(t // tm, n // tn),
        in_specs=[pl.BlockSpec((tm, d), lambda i, j: (i, 0)),
                  pl.BlockSpec((1, 6, d), lambda i, j: (seq(i), 0, 0)),
                  pl.BlockSpec((d, tn), lambda i, j: (0, j))],
        out_specs=pl.BlockSpec((tm, tn), lambda i, j: (i, j)),
        out_shape=jax.ShapeDtypeStruct((t, n), F32),
        scratch_shapes=[pltpu.VMEM((tm, d), BF16)],
        compiler_params=pltpu.CompilerParams(
            dimension_semantics=("arbitrary", "arbitrary"), vmem_limit_bytes=VMEM_LIMIT),
    )(x, mod, w)


def _ln_rows(x, g, b):
    mu = jnp.mean(x, -1, keepdims=True)
    xc = x - mu
    var = jnp.mean(xc * xc, -1, keepdims=True)
    return xc * lax.rsqrt(var + LN_EPS) * g + b


def _mm_resln_kernel(a_ref, w_ref, x_ref, mod_ref, g_ref, b_ref, o_ref, *, gate_row):
    f = jnp.dot(a_ref[...], w_ref[...], preferred_element_type=F32)
    o_ref[...] = _ln_rows(ALPHA * x_ref[...] + mod_ref[0, gate_row:gate_row + 1, :] * f, g_ref[...], b_ref[...])


def mm_resln(a, w, x, mod, gate_row, ln_g, ln_b, seq_len):
    t, k = a.shape
    d = w.shape[1]
    tm = 256
    seq = _seq_of_block(tm, seq_len, mod.shape[0])
    const = lambda i: (0, 0)
    return pl.pallas_call(
        functools.partial(_mm_resln_kernel, gate_row=gate_row),
        grid=(t // tm,),
        in_specs=[pl.BlockSpec((tm, k), lambda i: (i, 0)),
                  pl.BlockSpec((k, d), const, pipeline_mode=pl.Buffered(1)),
                  pl.BlockSpec((tm, d), lambda i: (i, 0)),
                  pl.BlockSpec((1, 6, d), lambda i: (seq(i), 0, 0)),
                  pl.BlockSpec((1, d), const), pl.BlockSpec((1, d), const)],
        out_specs=pl.BlockSpec((tm, d), lambda i: (i, 0)),
        out_shape=jax.ShapeDtypeStruct((t, d), F32),
        compiler_params=pltpu.CompilerParams(
            dimension_semantics=("arbitrary",), vmem_limit_bytes=VMEM_LIMIT),
    )(a, w, x, mod, ln_g, ln_b)


def _branch_mix_kernel(yf_ref, yb_ref, g_ref, bonus_ref, u_ref, ga_ref, gb_ref,
                       lnxg_ref, lnxb_ref, cvg_ref, cvb_ref, woa_ref, wob_ref, ones_ref, o_ref):
    ones = ones_ref[...]
    cw = ones.shape[0]

    def head_mean(x):
        parts = [_split_dot(x[:, c:c + cw], ones) for c in range(0, C_A, cw)]
        return jnp.concatenate(parts, axis=1) * (1.0 / HEAD_DIM)

    y = yf_ref[...] + yb_ref[...]
    yc = y - head_mean(y)
    yn = yc * lax.rsqrt(head_mean(yc * yc) + GN_EPS) * lnxg_ref[...] + lnxb_ref[...]
    ya = jnp.dot(((yn + bonus_ref[...]) * g_ref[...]).astype(BF16), woa_ref[...], preferred_element_type=F32)
    ub = _ln_rows(u_ref[...], cvg_ref[...], cvb_ref[...])
    ub = ub * jax.nn.sigmoid(ub)
    yb = jnp.dot(ub.astype(BF16), wob_ref[...], preferred_element_type=F32)
    o_ref[...] = (jax.nn.sigmoid(ga_ref[...]) * ya + jax.nn.sigmoid(gb_ref[...]) * yb).astype(BF16)


def branch_mix(yf, yb, g, bonus, uconv, z, p):
    t = yf.shape[0]
    tm = 256
    const = lambda i: (0, 0)
    act = pl.BlockSpec((tm, C_A), lambda i: (i, 0))
    vec = pl.BlockSpec((1, C_A), const)
    wspec = pl.BlockSpec((C_A, D_MODEL), const, pipeline_mode=pl.Buffered(1))
    return pl.pallas_call(
        _branch_mix_kernel,
        grid=(t // tm,),
        in_specs=[act, act, act, act, act,
                  pl.BlockSpec((tm, D_MODEL), lambda i: (i, Z_GA // D_MODEL)),
                  pl.BlockSpec((tm, D_MODEL), lambda i: (i, Z_GB // D_MODEL)),
                  vec, vec, vec, vec, wspec, wspec,
                  pl.BlockSpec((2 * LANES, 2 * LANES), const)],
        out_specs=pl.BlockSpec((tm, D_MODEL), lambda i: (i, 0)),
        out_shape=jax.ShapeDtypeStruct((t, D_MODEL), BF16),
        compiler_params=pltpu.CompilerParams(
            dimension_semantics=("arbitrary",), vmem_limit_bytes=VMEM_LIMIT),
    )(yf, yb, g, bonus, uconv, z, z, p["tm_lnx_g"], p["tm_lnx_b"], p["cv_ln_g"], p["cv_ln_b"],
      p["w_o_a"], p["w_o_b"], _head_ones(2 * LANES))


def _mod_kernel(c_ref, w_ref, b_ref, o_ref):
    o_ref[0] = jnp.dot(c_ref[...], w_ref[0].astype(BF16), preferred_element_type=F32) + b_ref[0]


def modulation(cond, w_mod, b_mod):
    depth, d, n = w_mod.shape
    tn = 1024
    return pl.pallas_call(
        _mod_kernel,
        grid=(depth, n // tn),
        in_specs=[pl.BlockSpec((MOD_ROWS, d), lambda l, j: (0, 0)),
                  pl.BlockSpec((1, d, tn), lambda l, j: (l, 0, j)),
                  pl.BlockSpec((1, 1, tn), lambda l, j: (l, 0, j))],
        out_specs=pl.BlockSpec((1, MOD_ROWS, tn), lambda l, j: (l, 0, j)),
        out_shape=jax.ShapeDtypeStruct((depth, MOD_ROWS, n), F32),
        compiler_params=pltpu.CompilerParams(
            dimension_semantics=("arbitrary", "arbitrary"), vmem_limit_bytes=VMEM_LIMIT),
    )(cond, w_mod, b_mod.reshape(depth, 1, n))


def _path_layer(x, mod, s0, p, bsz, seq_len, grid):
    t = x.shape[0]
    z = mod_matmul(x, mod, p["w_in"], 0, 1, seq_len)
    r, v, kk, g, bonus, w2, k2, b2 = rwkv_pre(z, p["tm_mu"], p, seq_len, grid)
    seq3 = (bsz, seq_len, C_A)
    seq4 = (2, bsz, seq_len, C_A)
    yf, yb, s_out = wkv_scan(r.reshape(seq3), v.reshape(seq3), kk.reshape(seq3),
                             w2.reshape(seq4), k2.reshape(seq4), b2.reshape(seq4), s0)
    uconv = glu_conv(z, p["cv_w"], p["cv_b"], seq_len)
    ymix = branch_mix(yf.reshape(t, C_A), yb.reshape(t, C_A), g, bonus, uconv, z, p)
    x = mm_resln(ymix, p["w_out"], x, mod, 2, p["ln1_g"], p["ln1_b"], seq_len)
    up = mod_matmul(x, mod, p["w_up"], 3, 4, seq_len)
    act = ffn_act(up, p["ffn_cv_w"], p["ffn_cv_b"], seq_len)
    x = mm_resln(act, p["w_down"], x, mod, 5, p["ln2_g"], p["ln2_b"], seq_len)
    return x, s_out


def _pad_cols(x, n):
    return jnp.pad(x, ((0, 0), (0, n - x.shape[1])))


def _pair_blocks(w):
    return jnp.transpose(w.reshape(w.shape[0], HEAD_PAIRS, LANES), (1, 0, 2))


def _layer_params(l, w_in, tm_mu, tm_w0, tm_w_b, tm_a0, tm_a_b, tm_g_b, tm_k_k, tm_k_a, tm_r_k, tm_lnx_g,
                  tm_lnx_b, w_o_a, cv_w, cv_b, cv_ln_g, cv_ln_b, w_o_b, w_out, ln1_g, ln1_b, w_up,
                  ffn_cv_w, ffn_cv_b, w_down, ln2_g, ln2_b):
    o_lora = 3 * C_A
    o_a = C_RW
    o_b = C_RW + C_B
    o_ga = C_RW + 2 * C_B
    o_gb = o_ga + D_MODEL

    def permute_z(w):
        return jnp.concatenate([w[:, :o_lora], w[:, o_a:o_b], w[:, o_ga:o_gb], w[:, o_gb:], w[:, o_b:o_ga],
                                _pad_cols(w[:, o_lora:o_a], LORA_W)], axis=1)

    def pad_ff(w):
        return jnp.concatenate([_pad_cols(w[:, :D_FF], FF_PAD), _pad_cols(w[:, D_FF:], FF_PAD)], axis=1)

    def dir_blocks(w):
        w0, w1 = _pair_blocks(w[0]), _pair_blocks(w[1])
        zero = jnp.zeros_like(w0)
        return jnp.concatenate([jnp.concatenate([w0, zero], axis=2), jnp.concatenate([zero, w1], axis=2)], axis=1)

    mu = tm_mu[l][None]
    mu_z = jnp.concatenate([mu[:, :o_lora], jnp.zeros((1, Z_LORA - o_lora), F32),
                            _pad_cols(mu[:, o_lora:], LORA_W)], axis=1)
    return {
        "w_in": permute_z(w_in[l]).astype(BF16),
        "tm_mu": mu_z,
        "tm_w0": tm_w0[l], "tm_a0": tm_a0[l],
        "tm_w_b": dir_blocks(tm_w_b[l]).astype(BF16),
        "tm_a_b": dir_blocks(tm_a_b[l]).astype(BF16),
        "tm_g_b": _pair_blocks(jnp.pad(tm_g_b[l], ((0, 2 * LANES - GATE_LORA), (0, 0)))).astype(BF16),
        "tm_k_k": tm_k_k[l][None], "tm_k_a": tm_k_a[l][None], "tm_r_k": tm_r_k[l].reshape(1, C_A),
        "tm_lnx_g": tm_lnx_g[l][None], "tm_lnx_b": tm_lnx_b[l][None],
        "w_o_a": w_o_a[l].astype(BF16), "w_o_b": w_o_b[l].astype(BF16), "w_out": w_out[l].astype(BF16),
        "cv_w": cv_w[l], "cv_b": cv_b[l][None], "cv_ln_g": cv_ln_g[l][None], "cv_ln_b": cv_ln_b[l][None],
        "ln1_g": ln1_g[l][None], "ln1_b": ln1_b[l][None], "ln2_g": ln2_g[l][None], "ln2_b": ln2_b[l][None],
        "w_up": pad_ff(w_up[l]).astype(BF16),
        "ffn_cv_w": pad_ff(ffn_cv_w[l]), "ffn_cv_b": pad_ff(ffn_cv_b[l][None]),
        "w_down": jnp.pad(w_down[l], ((0, FF_PAD - D_FF), (0, 0))).astype(BF16),
    }


def kernel(x_prompt, x_sample, state_rwkv, c, c_ctx, w_mod, b_mod, w_in, tm_mu, tm_w0, tm_w_b, tm_a0,
           tm_a_b, tm_g_b, tm_k_k, tm_k_a, tm_r_k, tm_lnx_g, tm_lnx_b, w_o_a, cv_w, cv_b, cv_ln_g,
           cv_ln_b, w_o_b, w_out, ln1_g, ln1_b, w_up, ffn_cv_w, ffn_cv_b, w_down, ln2_g, ln2_b):
    bp, lp, d = x_prompt.shape
    bs, ls, _ = x_sample.shape
    xp = x_prompt.reshape(bp * lp, d)
    xs = x_sample.reshape(bs * ls, d)
    cond = jnp.concatenate([c_ctx[None], c, jnp.zeros((MOD_ROWS - 1 - bs, d), F32)], axis=0)
    mods = modulation(jax.nn.silu(cond).astype(BF16), w_mod, b_mod)
    s_ctx0 = jnp.zeros((bp, 2, HEAD_PAIRS, HEAD_DIM, LANES), F32)
    new_states = []
    for l in range(DEPTH):
        p = _layer_params(l, w_in, tm_mu, tm_w0, tm_w_b, tm_a0, tm_a_b, tm_g_b, tm_k_k, tm_k_a, tm_r_k,
                          tm_lnx_g, tm_lnx_b, w_o_a, cv_w, cv_b, cv_ln_g, cv_ln_b, w_o_b, w_out, ln1_g,
                          ln1_b, w_up, ffn_cv_w, ffn_cv_b, w_down, ln2_g, ln2_b)
        mod = mods[l].reshape(MOD_ROWS, 6, d)
        xp, s_l = _path_layer(xp, mod[0:1], s_ctx0, p, bp, lp, False)
        xs, _ = _path_layer(xs, mod[1:1 + bs], _state_to_tiles(state_rwkv[:, l]), p, bs, ls, True)
        new_states.append(_tiles_to_state(s_l))
    return (xp.reshape(bp, lp, d), xs.reshape(bs, ls, d), jnp.stack(new_states, axis=1))
```

```python
import functools
import math

import jax
import jax.numpy as jnp
from jax import lax
from jax.experimental import pallas as pl
from jax.experimental.pallas import tpu as pltpu

D_MODEL = 2048
DEPTH = 4
GRID_W = 64
HEAD_DIM = 64
C_A = D_MODEL // 2
H_A = C_A // HEAD_DIM
C_B = D_MODEL // 2
DECAY_LORA = 64
ICLR_LORA = 64
GATE_LORA = 160
CONV_K = 31
FFN_CONV_K = 3
D_FF = ((8 * D_MODEL // 3 + 127) // 128) * 128
C_RW = 3 * C_A + 2 * DECAY_LORA + 2 * ICLR_LORA + GATE_LORA
ALPHA = (2 * DEPTH) ** 0.25
LN_EPS = 1e-5
GN_EPS = HEAD_DIM * 1e-5

LANES = 128
SUBLANES = 8
HEAD_PAIRS = C_A // LANES
SCAN_T = 128
VMEM_LIMIT = 56 * 1024 * 1024
ROW_BLOCK = 2048
DECAY_SCALE = math.exp(-0.5)
MOD_ROWS = 16

Z_R, Z_K, Z_V, Z_A = 0, C_A, 2 * C_A, 3 * C_A
Z_GA = Z_A + C_B
Z_GB = Z_GA + D_MODEL
Z_B = Z_GB + D_MODEL
Z_LORA = Z_B + C_B
LORA_W = 4 * LANES
C_Z = Z_LORA + LORA_W
FF_PAD = ((D_FF + 4 * LANES - 1) // (4 * LANES)) * (4 * LANES)

BF16 = jnp.bfloat16
F32 = jnp.float32


def _scan_kernel(rf_ref, vf_ref, kkf_ref, wf_ref, kf_ref, bf_ref,
                 rb_ref, vb_ref, kkb_ref, wb_ref, kb_ref, bb_ref, s0_ref, bd_ref,
                 yf_ref, yb_ref, sout_ref, s_scr, vt_scr, y_scr, rows_scr, *, nb):
    j = pl.program_id(1)
    nj = pl.num_programs(1)
    dirs = ((rf_ref, vf_ref, kkf_ref, wf_ref, kf_ref, bf_ref, yf_ref),
            (rb_ref, vb_ref, kkb_ref, wb_ref, kb_ref, bb_ref, yb_ref))
    chains = [(d, n, p) for d in range(2) for n in range(nb) for p in range(HEAD_PAIRS)]
    tile3 = (HEAD_DIM // SUBLANES, SUBLANES, LANES)
    R_, KK_, W_, K_, B_ = range(5)
    lane = lax.broadcasted_iota(jnp.int32, (HEAD_DIM, LANES), 1)
    low_half = lane < HEAD_DIM
    lane_in_head = lane & (HEAD_DIM - 1)

    def cid(d, n, p):
        return (d * nb + n) * HEAD_PAIRS + p

    @pl.when(j == 0)
    def _():
        for d, n, p in chains:
            s_scr[cid(d, n, p)] = s0_ref[n, d, p]

    for d, n, p in chains:
        c = cid(d, n, p)
        vt = dirs[d][1][n, :, p * LANES:(p + 1) * LANES].T
        h0, h1 = vt[:HEAD_DIM], vt[HEAD_DIM:]
        vt_scr[c, 0] = jnp.where(low_half, h0, pltpu.roll(h1, HEAD_DIM, 1))
        vt_scr[c, 1] = jnp.where(low_half, pltpu.roll(h0, HEAD_DIM, 1), h1)
        y_scr[c] = jnp.zeros((2, HEAD_DIM, LANES), F32)

    for d in range(2):
        for k, src in enumerate((dirs[d][0], dirs[d][2], dirs[d][3], dirs[d][4], dirs[d][5])):
            for n in range(nb):
                blk = src[n] if k < 2 else src[0, n]
                rows_scr[d * 5 + k, n] = blk.reshape(SCAN_T, HEAD_PAIRS, LANES)

    def row(d, k, n, t, p):
        return rows_scr[d * 5 + k, n, t, pl.ds(p, 1), :][None]

    def weighted(c, r_row, kk_row):
        s = s_scr[c].reshape(tile3)
        return jnp.concatenate(
            [(s * r_row).reshape(HEAD_DIM, LANES), (s * kk_row).reshape(HEAD_DIM, LANES)], axis=1).astype(BF16)

    def step(i, carry):
        ts = (i, SCAN_T - 1 - i)
        tprev = (i - 1, SCAN_T - i)
        lhs = []
        for d, n, p in chains:
            c = cid(d, n, p)
            tp = jnp.clip(tprev[d], 0, SCAN_T - 1)
            lhs.append(weighted(c, row(d, R_, n, tp, p), row(d, KK_, n, ts[d], p)))
        for d, n, p in chains:
            if p % 2:
                continue
            c = cid(d, n, p)
            at_t = lane_in_head == (ts[d] & (HEAD_DIM - 1))
            half = ts[d] // HEAD_DIM
            lhs.append(jnp.concatenate([jnp.where(at_t, vt_scr[c, half], 0.0),
                                        jnp.where(at_t, vt_scr[c + 1, half], 0.0)], axis=1).astype(BF16))
        res = jnp.dot(jnp.concatenate(lhs, axis=0), bd_ref[...], preferred_element_type=F32)
        nch = len(chains)
        for d, n, p in chains:
            c = cid(d, n, p)
            t = ts[d]
            tp = jnp.clip(tprev[d], 0, SCAN_T - 1)
            at_prev = (lane_in_head == (tp & (HEAD_DIM - 1))) & (i > 0)
            hp = tp // HEAD_DIM
            rc = res[c * HEAD_DIM:(c + 1) * HEAD_DIM]
            y_scr[c, hp] = jnp.where(at_prev, rc[:, :LANES], y_scr[c, hp])
            skk = rc[:, LANES:].reshape(tile3)
            vrow = (nch + c // 2) * HEAD_DIM
            vcol = res[vrow:vrow + HEAD_DIM, (c % 2) * LANES:(c % 2 + 1) * LANES].reshape(tile3)
            s = s_scr[c].reshape(tile3)
            s = s * row(d, W_, n, t, p) - skk * row(d, B_, n, t, p) + vcol * row(d, K_, n, t, p)
            s_scr[c] = s.reshape(HEAD_DIM, LANES)
        return carry

    lax.fori_loop(0, SCAN_T, step, 0)

    tlast = (SCAN_T - 1, 0)
    for d, n, p in chains:
        c = cid(d, n, p)
        r_row = row(d, R_, n, tlast[d], p)
        yl = jnp.dot(weighted(c, r_row, r_row), bd_ref[...], preferred_element_type=F32)[:, :LANES]
        hl = tlast[d] // HEAD_DIM
        at_last = lane_in_head == (tlast[d] & (HEAD_DIM - 1))
        lo = y_scr[c, 0]
        hi = y_scr[c, 1]
        if hl == 0:
            lo = jnp.where(at_last, yl, lo)
        else:
            hi = jnp.where(at_last, yl, hi)
        y0 = jnp.where(low_half, lo, pltpu.roll(hi, HEAD_DIM, 1))
        y1 = jnp.where(low_half, pltpu.roll(lo, HEAD_DIM, 1), hi)
        dirs[d][6][n, :, p * LANES:(p + 1) * LANES] = jnp.concatenate([y0, y1], axis=0).T

    @pl.when(j == nj - 1)
    def _():
        for d, n, p in chains:
            s = s_scr[cid(d, n, p)]
            sout_ref[n, d, 2 * p] = s[:, :HEAD_DIM]
            sout_ref[n, d, 2 * p + 1] = s[:, HEAD_DIM:]


def _head_ones(n):
    i = jnp.arange(n)
    return (i[:, None] // HEAD_DIM == i[None, :] // HEAD_DIM).astype(BF16)


def wkv_scan(r, v, kk, w2, k2, b2, s0):
    bsz, seq, _ = r.shape
    nb = 2
    nj = seq // SCAN_T

    def flat(rev):
        return pl.BlockSpec((nb, SCAN_T, C_A), lambda b, j: (b, nj - 1 - j if rev else j, 0))

    def flat_dir(d):
        return pl.BlockSpec((1, nb, SCAN_T, C_A), lambda b, j: (d, b, nj - 1 - j if d else j, 0))

    state = pl.BlockSpec((nb, 2, HEAD_PAIRS, HEAD_DIM, LANES), lambda b, j: (b, 0, 0, 0, 0))

    nchains = 2 * nb * HEAD_PAIRS
    return pl.pallas_call(
        functools.partial(_scan_kernel, nb=nb),
        name="wkv_scan",
        grid=(bsz // nb, nj),
        in_specs=[flat(0), flat(0), flat(0), flat_dir(0), flat_dir(0), flat_dir(0),
                  flat(1), flat(1), flat(1), flat_dir(1), flat_dir(1), flat_dir(1),
                  state, pl.BlockSpec((2 * LANES, 2 * LANES), lambda b, j: (0, 0))],
        out_specs=[flat(0), flat(1),
                   pl.BlockSpec((nb, 2, H_A, HEAD_DIM, HEAD_DIM), lambda b, j: (b, 0, 0, 0, 0))],
        out_shape=[jax.ShapeDtypeStruct((bsz, seq, C_A), F32),
                   jax.ShapeDtypeStruct((bsz, seq, C_A), F32),
                   jax.ShapeDtypeStruct((bsz, 2, H_A, HEAD_DIM, HEAD_DIM), F32)],
        scratch_shapes=[pltpu.VMEM((nchains, HEAD_DIM, LANES), F32),
                        pltpu.VMEM((nchains, 2, HEAD_DIM, LANES), F32),
                        pltpu.VMEM((nchains, 2, HEAD_DIM, LANES), F32),
                        pltpu.VMEM((10, nb, SCAN_T, HEAD_PAIRS, LANES), F32)],
        compiler_params=pltpu.CompilerParams(
            dimension_semantics=("arbitrary", "arbitrary"),
            vmem_limit_bytes=VMEM_LIMIT),
    )(r, v, kk, w2, k2, b2, r, v, kk, w2, k2, b2, s0, _head_ones(2 * LANES))


def _state_to_tiles(s):
    b = s.shape[0]
    s = s.reshape(b, 2, HEAD_PAIRS, 2, HEAD_DIM, HEAD_DIM)
    return jnp.transpose(s, (0, 1, 2, 4, 3, 5)).reshape(b, 2, HEAD_PAIRS, HEAD_DIM, LANES)


def _rows_in_seq(shape, seq_len):
    return lax.broadcasted_iota(jnp.int32, shape, 0) & (seq_len - 1)


def _shift_rows(x, offset, rs, seq_len):
    rolled = pltpu.roll(x, (-offset) % x.shape[0], 0)
    ok = (rs >= -offset) if offset < 0 else (rs < seq_len - offset)
    return jnp.where(ok, rolled, 0.0)


def _token_shift(x, seq_len, grid):
    rs = _rows_in_seq(x.shape, seq_len)
    col = lax.broadcasted_iota(jnp.int32, x.shape, 1)
    if not grid:
        return jnp.where((col & 1) == 0, _shift_rows(x, -1, rs, seq_len), _shift_rows(x, 1, rs, seq_len))
    gc = rs & (GRID_W - 1)
    left = jnp.where(gc == 0, 0.0, pltpu.roll(x, 1, 0))
    right = jnp.where(gc == GRID_W - 1, 0.0, pltpu.roll(x, x.shape[0] - 1, 0))
    up = _shift_rows(x, -GRID_W, rs, seq_len)
    down = _shift_rows(x, GRID_W, rs, seq_len)
    q = col & 3
    return jnp.where(q == 0, left, jnp.where(q == 1, right, jnp.where(q == 2, up, down)))


def _split_dot(x, ones):
    hi = x.astype(BF16)
    lo = (x - hi.astype(F32)).astype(BF16)
    return (jnp.dot(hi, ones, preferred_element_type=F32) + jnp.dot(lo, ones, preferred_element_type=F32))


def _rwkv_pre_kernel(zr_ref, zk_ref, zv_ref, zl_ref, mur_ref, muk_ref, muv_ref, mul_ref,
                     w0_ref, a0_ref, kkw_ref, kaw_ref, rkw_ref, ww_ref, wa_ref, wg_ref, ones_ref,
                     r_ref, v_ref, kk_ref, g_ref, bonus_ref, w2_ref, k2_ref, b2_ref,
                     twl_scr, al_scr, sgl_scr, *, seq_len, grid):
    def mix(z, mu):
        return z + (_token_shift(z, seq_len, grid) - z) * mu

    @pl.when(pl.program_id(1) == 0)
    def _():
        zl = mix(zl_ref[...], mul_ref[...])
        twl_scr[...] = jnp.tanh(zl[:, :LANES]).astype(BF16)
        al_scr[...] = zl[:, LANES:2 * LANES].astype(BF16)
        sgl_scr[...] = jax.nn.sigmoid(zl[:, 2 * LANES:]).astype(BF16)

    r = mix(zr_ref[...], mur_ref[...])
    k = mix(zk_ref[...], muk_ref[...])
    v = mix(zv_ref[...], muv_ref[...])
    ones = ones_ref[...]
    kx = k * kkw_ref[...]
    kk = kx / jnp.maximum(jnp.sqrt(_split_dot(kx * kx, ones)), 1e-12)
    wd = jnp.dot(twl_scr[...], ww_ref[0], preferred_element_type=F32)
    ad = jnp.dot(al_scr[...], wa_ref[0], preferred_element_type=F32)
    ksum = None
    for d in range(2):
        cols = slice(d * LANES, (d + 1) * LANES)
        w2_ref[d] = jnp.exp(-DECAY_SCALE * jax.nn.sigmoid(wd[:, cols] + w0_ref[d:d + 1, :]))
        a_d = jax.nn.sigmoid(ad[:, cols] + a0_ref[d:d + 1, :])
        k_d = k * (1.0 + (a_d - 1.0) * kaw_ref[...])
        k2_ref[d] = k_d
        b2_ref[d] = kk * a_d
        ksum = k_d if ksum is None else ksum + k_d
    r_ref[...] = r
    v_ref[...] = v
    kk_ref[...] = kk
    g_ref[...] = jnp.dot(sgl_scr[...], wg_ref[0], preferred_element_type=F32)
    bonus_ref[...] = _split_dot(r * ksum * rkw_ref[...], ones) * v


def rwkv_pre(z, mu, p, seq_len, grid):
    t = z.shape[0]
    rb = ROW_BLOCK

    def zcol(off):
        return pl.BlockSpec((rb, LANES), lambda i, q: (i, off + q))

    def vcol(off):
        return pl.BlockSpec((1, LANES), lambda i, q: (0, off + q))

    def pair_vec(rows):
        return pl.BlockSpec((rows, LANES), lambda i, q: (0, q))

    def pair_w(a, b):
        return pl.BlockSpec((1, a, b), lambda i, q: (q, 0, 0))

    lora_blk = Z_LORA // LORA_W
    out1 = pl.BlockSpec((rb, LANES), lambda i, q: (i, q))
    out2 = pl.BlockSpec((2, rb, LANES), lambda i, q: (0, i, q))
    s1 = jax.ShapeDtypeStruct((t, C_A), F32)
    s2 = jax.ShapeDtypeStruct((2, t, C_A), F32)
    return pl.pallas_call(
        functools.partial(_rwkv_pre_kernel, seq_len=seq_len, grid=grid),
        name="rwkv_pre",
        grid=(t // rb, HEAD_PAIRS),
        in_specs=[zcol(Z_R // LANES), zcol(Z_K // LANES), zcol(Z_V // LANES),
                  pl.BlockSpec((rb, LORA_W), lambda i, q: (i, lora_blk)),
                  vcol(Z_R // LANES), vcol(Z_K // LANES), vcol(Z_V // LANES),
                  pl.BlockSpec((1, LORA_W), lambda i, q: (0, lora_blk)),
                  pair_vec(2), pair_vec(2), pair_vec(1), pair_vec(1), pair_vec(1),
                  pair_w(LANES, 2 * LANES), pair_w(LANES, 2 * LANES), pair_w(2 * LANES, LANES),
                  pl.BlockSpec((LANES, LANES), lambda i, q: (0, 0))],
        out_specs=[out1, out1, out1, out1, out1, out2, out2, out2],
        out_shape=[s1, s1, s1, s1, s1, s2, s2, s2],
        scratch_shapes=[pltpu.VMEM((rb, LANES), BF16), pltpu.VMEM((rb, LANES), BF16),
                        pltpu.VMEM((rb, 2 * LANES), BF16)],
        compiler_params=pltpu.CompilerParams(
            dimension_semantics=("arbitrary", "arbitrary"), vmem_limit_bytes=VMEM_LIMIT),
    )(z, z, z, z, mu, mu, mu, mu, p["tm_w0"], p["tm_a0"], p["tm_k_k"], p["tm_k_a"], p["tm_r_k"],
      p["tm_w_b"], p["tm_a_b"], p["tm_g_b"], _head_ones(LANES))


def _glu_conv_kernel(a_ref, b_ref, w_ref, cb_ref, o_ref, *, seq_len):
    u = a_ref[...] * jax.nn.sigmoid(b_ref[...])
    rs = _rows_in_seq(u.shape, seq_len)
    half = (CONV_K - 1) // 2
    acc = cb_ref[...] + u * w_ref[half:half + 1, :]
    for k in range(CONV_K):
        if k != half:
            acc = acc + _shift_rows(u, k - half, rs, seq_len) * w_ref[k:k + 1, :]
    o_ref[...] = acc


def glu_conv(z, cv_w, cv_b, seq_len):
    t = z.shape[0]
    rb, cw = ROW_BLOCK, 2 * LANES
    return pl.pallas_call(
        functools.partial(_glu_conv_kernel, seq_len=seq_len),
        name="glu_conv",
        grid=(t // rb, C_B // cw),
        in_specs=[pl.BlockSpec((rb, cw), lambda i, j: (i, Z_A // cw + j)),
                  pl.BlockSpec((rb, cw), lambda i, j: (i, Z_B // cw + j)),
                  pl.BlockSpec((CONV_K, cw), lambda i, j: (0, j)),
                  pl.BlockSpec((1, cw), lambda i, j: (0, j))],
        out_specs=pl.BlockSpec((rb, cw), lambda i, j: (i, j)),
        out_shape=jax.ShapeDtypeStruct((t, C_B), F32),
        compiler_params=pltpu.CompilerParams(
            dimension_semantics=("arbitrary", "arbitrary"), vmem_limit_bytes=VMEM_LIMIT),
    )(z, z, cv_w, cv_b)


def _ffn_act_kernel(g_ref, v_ref, wg_ref, wv_ref, bg_ref, bv_ref, o_ref, *, seq_len):
    rs = _rows_in_seq(g_ref.shape, seq_len)

    def conv3(x, w_ref, b_ref):
        return (b_ref[...] + _shift_rows(x, -1, rs, seq_len) * w_ref[0:1, :] + x * w_ref[1:2, :]
                + _shift_rows(x, 1, rs, seq_len) * w_ref[2:3, :])

    ug = conv3(g_ref[...], wg_ref, bg_ref)
    uv = conv3(v_ref[...], wv_ref, bv_ref)
    o_ref[...] = (ug * jax.nn.sigmoid(ug) * uv).astype(BF16)


def ffn_act(up, cw_, cb_, seq_len):
    t = up.shape[0]
    rb, cw = ROW_BLOCK, 4 * LANES
    nj = FF_PAD // cw
    return pl.pallas_call(
        functools.partial(_ffn_act_kernel, seq_len=seq_len),
        name="ffn_act",
        grid=(t // rb, nj),
        in_specs=[pl.BlockSpec((rb, cw), lambda i, j: (i, j)),
                  pl.BlockSpec((rb, cw), lambda i, j: (i, nj + j)),
                  pl.BlockSpec((FFN_CONV_K, cw), lambda i, j: (0, j)),
                  pl.BlockSpec((FFN_CONV_K, cw), lambda i, j: (0, nj + j)),
                  pl.BlockSpec((1, cw), lambda i, j: (0, j)),
                  pl.BlockSpec((1, cw), lambda i, j: (0, nj + j))],
        out_specs=pl.BlockSpec((rb, cw), lambda i, j: (i, j)),
        out_shape=jax.ShapeDtypeStruct((t, FF_PAD), BF16),
        compiler_params=pltpu.CompilerParams(
            dimension_semantics=("arbitrary", "arbitrary"), vmem_limit_bytes=VMEM_LIMIT),
    )(up, up, cw_, cw_, cb_, cb_)


def _seq_of_block(tm, seq_len, nseq):
    if nseq == 1:
        return lambda i: 0
    return lambda i: (i * tm) // seq_len


def _mod_mm_kernel(x_ref, mod_ref, w_ref, o_ref, h_scr, *, shift_row, scale_row):
    @pl.when(pl.program_id(1) == 0)
    def _():
        h = x_ref[...] * (1.0 + mod_ref[0, scale_row:scale_row + 1, :]) + mod_ref[0, shift_row:shift_row + 1, :]
        h_scr[...] = h.astype(BF16)

    o_ref[...] = jnp.dot(h_scr[...], w_ref[...], preferred_element_type=F32)


def mod_matmul(x, mod, w, shift_row, scale_row, seq_len):
    t, d = x.shape
    n = w.shape[1]
    tm, tn = 1024, 512
    seq = _seq_of_block(tm, seq_len, mod.shape[0])
    return pl.pallas_call(
        functools.partial(_mod_mm_kernel, shift_row=shift_row, scale_row=scale_row),
        name="mod_matmul",
        grid=(t // tm, n // tn),
        in_specs=[pl.BlockSpec((tm, d), lambda i, j: (i, 0)),
                  pl.BlockSpec((1, 6, d), lambda i, j: (seq(i), 0, 0)),
                  pl.BlockSpec((d, tn), lambda i, j: (0, j))],
        out_specs=pl.BlockSpec((tm, tn), lambda i, j: (i, j)),
        out_shape=jax.ShapeDtypeStruct((t, n), F32),
        scratch_shapes=[pltpu.VMEM((tm, d), BF16)],
        compiler_params=pltpu.CompilerParams(
            dimension_semantics=("arbitrary", "arbitrary"), vmem_limit_bytes=VMEM_LIMIT),
    )(x, mod, w)


def _ln_rows(x, g, b):
    mu = jnp.mean(x, -1, keepdims=True)
    xc = x - mu
    var = jnp.mean(xc * xc, -1, keepdims=True)
    return xc * lax.rsqrt(var + LN_EPS) * g + b


def _mm_resln_kernel(a_ref, w_ref, x_ref, mod_ref, g_ref, b_ref, o_ref, *, gate_row):
    f = jnp.dot(a_ref[...], w_ref[...], preferred_element_type=F32)
    o_ref[...] = _ln_rows(ALPHA * x_ref[...] + mod_ref[0, gate_row:gate_row + 1, :] * f, g_ref[...], b_ref[...])


def mm_resln(a, w, x, mod, gate_row, ln_g, ln_b, seq_len):
    t, k = a.shape
    d = w.shape[1]
    tm = 256
    seq = _seq_of_block(tm, seq_len, mod.shape[0])
    const = lambda i: (0, 0)
    return pl.pallas_call(
        functools.partial(_mm_resln_kernel, gate_row=gate_row),
        name="mm_resln",
        grid=(t // tm,),
        in_specs=[pl.BlockSpec((tm, k), lambda i: (i, 0)),
                  pl.BlockSpec((k, d), const, pipeline_mode=pl.Buffered(1)),
                  pl.BlockSpec((tm, d), lambda i: (i, 0)),
                  pl.BlockSpec((1, 6, d), lambda i: (seq(i), 0, 0)),
                  pl.BlockSpec((1, d), const), pl.BlockSpec((1, d), const)],
        out_specs=pl.BlockSpec((tm, d), lambda i: (i, 0)),
        out_shape=jax.ShapeDtypeStruct((t, d), F32),
        compiler_params=pltpu.CompilerParams(
            dimension_semantics=("arbitrary",), vmem_limit_bytes=VMEM_LIMIT),
    )(a, w, x, mod, ln_g, ln_b)


def _branch_mix_kernel(yf_ref, yb_ref, g_ref, bonus_ref, u_ref, ga_ref, gb_ref,
                       lnxg_ref, lnxb_ref, cvg_ref, cvb_ref, woa_ref, wob_ref, ones_ref, o_ref):
    ones = ones_ref[...]
    cw = ones.shape[0]

    def head_mean(x):
        parts = [_split_dot(x[:, c:c + cw], ones) for c in range(0, C_A, cw)]
        return jnp.concatenate(parts, axis=1) * (1.0 / HEAD_DIM)

    y = yf_ref[...] + yb_ref[...]
    yc = y - head_mean(y)
    yn = yc * lax.rsqrt(head_mean(yc * yc) + GN_EPS) * lnxg_ref[...] + lnxb_ref[...]
    ya = jnp.dot(((yn + bonus_ref[...]) * g_ref[...]).astype(BF16), woa_ref[...], preferred_element_type=F32)
    ub = _ln_rows(u_ref[...], cvg_ref[...], cvb_ref[...])
    ub = ub * jax.nn.sigmoid(ub)
    yb = jnp.dot(ub.astype(BF16), wob_ref[...], preferred_element_type=F32)
    o_ref[...] = (jax.nn.sigmoid(ga_ref[...]) * ya + jax.nn.sigmoid(gb_ref[...]) * yb).astype(BF16)


def branch_mix(yf, yb, g, bonus, uconv, z, p):
    t = yf.shape[0]
    tm = 256
    const = lambda i: (0, 0)
    act = pl.BlockSpec((tm, C_A), lambda i: (i, 0))
    vec = pl.BlockSpec((1, C_A), const)
    wspec = pl.BlockSpec((C_A, D_MODEL), const, pipeline_mode=pl.Buffered(1))
    return pl.pallas_call(
        _branch_mix_kernel,
        name="branch_mix",
        grid=(t // tm,),
        in_specs=[act, act, act, act, act,
                  pl.BlockSpec((tm, D_MODEL), lambda i: (i, Z_GA // D_MODEL)),
                  pl.BlockSpec((tm, D_MODEL), lambda i: (i, Z_GB // D_MODEL)),
                  vec, vec, vec, vec, wspec, wspec,
                  pl.BlockSpec((2 * LANES, 2 * LANES), const)],
        out_specs=pl.BlockSpec((tm, D_MODEL), lambda i: (i, 0)),
        out_shape=jax.ShapeDtypeStruct((t, D_MODEL), BF16),
        compiler_params=pltpu.CompilerParams(
            dimension_semantics=("arbitrary",), vmem_limit_bytes=VMEM_LIMIT),
    )(yf, yb, g, bonus, uconv, z, z, p["tm_lnx_g"], p["tm_lnx_b"], p["cv_ln_g"], p["cv_ln_b"],
      p["w_o_a"], p["w_o_b"], _head_ones(2 * LANES))


def _mod_kernel(c_ref, w_ref, b_ref, o_ref):
    o_ref[0] = jnp.dot(c_ref[...], w_ref[0].astype(BF16), preferred_element_type=F32) + b_ref[0]


def modulation(cond, w_mod, b_mod):
    depth, d, n = w_mod.shape
    tn = 1024
    return pl.pallas_call(
        _mod_kernel,
        name="modulation",
        grid=(depth, n // tn),
        in_specs=[pl.BlockSpec((MOD_ROWS, d), lambda l, j: (0, 0)),
                  pl.BlockSpec((1, d, tn), lambda l, j: (l, 0, j)),
                  pl.BlockSpec((1, 1, tn), lambda l, j: (l, 0, j))],
        out_specs=pl.BlockSpec((1, MOD_ROWS, tn), lambda l, j: (l, 0, j)),
        out_shape=jax.ShapeDtypeStruct((depth, MOD_ROWS, n), F32),
        compiler_params=pltpu.CompilerParams(
            dimension_semantics=("arbitrary", "arbitrary"), vmem_limit_bytes=VMEM_LIMIT),
    )(cond, w_mod, b_mod.reshape(depth, 1, n))


def _path_layer(x, mod, s0, p, bsz, seq_len, grid):
    t = x.shape[0]
    z = mod_matmul(x, mod, p["w_in"], 0, 1, seq_len)
    r, v, kk, g, bonus, w2, k2, b2 = rwkv_pre(z, p["tm_mu"], p, seq_len, grid)
    seq3 = (bsz, seq_len, C_A)
    seq4 = (2, bsz, seq_len, C_A)
    yf, yb, s_out = wkv_scan(r.reshape(seq3), v.reshape(seq3), kk.reshape(seq3),
                             w2.reshape(seq4), k2.reshape(seq4), b2.reshape(seq4), s0)
    uconv = glu_conv(z, p["cv_w"], p["cv_b"], seq_len)
    ymix = branch_mix(yf.reshape(t, C_A), yb.reshape(t, C_A), g, bonus, uconv, z, p)
    x = mm_resln(ymix, p["w_out"], x, mod, 2, p["ln1_g"], p["ln1_b"], seq_len)
    up = mod_matmul(x, mod, p["w_up"], 3, 4, seq_len)
    act = ffn_act(up, p["ffn_cv_w"], p["ffn_cv_b"], seq_len)
    x = mm_resln(act, p["w_down"], x, mod, 5, p["ln2_g"], p["ln2_b"], seq_len)
    return x, s_out


def _pad_cols(x, n):
    return jnp.pad(x, ((0, 0), (0, n - x.shape[1])))


def _pair_blocks(w):
    return jnp.transpose(w.reshape(w.shape[0], HEAD_PAIRS, LANES), (1, 0, 2))


def _layer_params(l, w_in, tm_mu, tm_w0, tm_w_b, tm_a0, tm_a_b, tm_g_b, tm_k_k, tm_k_a, tm_r_k, tm_lnx_g,
                  tm_lnx_b, w_o_a, cv_w, cv_b, cv_ln_g, cv_ln_b, w_o_b, w_out, ln1_g, ln1_b, w_up,
                  ffn_cv_w, ffn_cv_b, w_down, ln2_g, ln2_b):
    o_lora = 3 * C_A
    o_a = C_RW
    o_b = C_RW + C_B
    o_ga = C_RW + 2 * C_B
    o_gb = o_ga + D_MODEL

    def permute_z(w):
        return jnp.concatenate([w[:, :o_lora], w[:, o_a:o_b], w[:, o_ga:o_gb], w[:, o_gb:], w[:, o_b:o_ga],
                                _pad_cols(w[:, o_lora:o_a], LORA_W)], axis=1)

    def pad_ff(w):
        return jnp.concatenate([_pad_cols(w[:, :D_FF], FF_PAD), _pad_cols(w[:, D_FF:], FF_PAD)], axis=1)

    def dir_blocks(w):
        w0, w1 = _pair_blocks(w[0]), _pair_blocks(w[1])
        zero = jnp.zeros_like(w0)
        return jnp.concatenate([jnp.concatenate([w0, zero], axis=2), jnp.concatenate([zero, w1], axis=2)], axis=1)

    mu = tm_mu[l][None]
    mu_z = jnp.concatenate([mu[:, :o_lora], jnp.zeros((1, Z_LORA - o_lora), F32),
                            _pad_cols(mu[:, o_lora:], LORA_W)], axis=1)
    return {
        "w_in": permute_z(w_in[l]).astype(BF16),
        "tm_mu": mu_z,
        "tm_w0": tm_w0[l], "tm_a0": tm_a0[l],
        "tm_w_b": dir_blocks(tm_w_b[l]).astype(BF16),
        "tm_a_b": dir_blocks(tm_a_b[l]).astype(BF16),
        "tm_g_b": _pair_blocks(jnp.pad(tm_g_b[l], ((0, 2 * LANES - GATE_LORA), (0, 0)))).astype(BF16),
        "tm_k_k": tm_k_k[l][None], "tm_k_a": tm_k_a[l][None], "tm_r_k": tm_r_k[l].reshape(1, C_A),
        "tm_lnx_g": tm_lnx_g[l][None], "tm_lnx_b": tm_lnx_b[l][None],
        "w_o_a": w_o_a[l].astype(BF16), "w_o_b": w_o_b[l].astype(BF16), "w_out": w_out[l].astype(BF16),
        "cv_w": cv_w[l], "cv_b": cv_b[l][None], "cv_ln_g": cv_ln_g[l][None], "cv_ln_b": cv_ln_b[l][None],
        "ln1_g": ln1_g[l][None], "ln1_b": ln1_b[l][None], "ln2_g": ln2_g[l][None], "ln2_b": ln2_b[l][None],
        "w_up": pad_ff(w_up[l]).astype(BF16),
        "ffn_cv_w": pad_ff(ffn_cv_w[l]), "ffn_cv_b": pad_ff(ffn_cv_b[l][None]),
        "w_down": jnp.pad(w_down[l], ((0, FF_PAD - D_FF), (0, 0))).astype(BF16),
    }


def kernel(x_prompt, x_sample, state_rwkv, c, c_ctx, w_mod, b_mod, w_in, tm_mu, tm_w0, tm_w_b, tm_a0,
           tm_a_b, tm_g_b, tm_k_k, tm_k_a, tm_r_k, tm_lnx_g, tm_lnx_b, w_o_a, cv_w, cv_b, cv_ln_g,
           cv_ln_b, w_o_b, w_out, ln1_g, ln1_b, w_up, ffn_cv_w, ffn_cv_b, w_down, ln2_g, ln2_b):
    bp, lp, d = x_prompt.shape
    bs, ls, _ = x_sample.shape
    xp = x_prompt.reshape(bp * lp, d)
    xs = x_sample.reshape(bs * ls, d)
    cond = jnp.concatenate([c_ctx[None], c, jnp.zeros((MOD_ROWS - 1 - bs, d), F32)], axis=0)
    mods = modulation(jax.nn.silu(cond).astype(BF16), w_mod, b_mod)
    s_ctx0 = jnp.zeros((bp, 2, HEAD_PAIRS, HEAD_DIM, LANES), F32)
    new_states = []
    for l in range(DEPTH):
        p = _layer_params(l, w_in, tm_mu, tm_w0, tm_w_b, tm_a0, tm_a_b, tm_g_b, tm_k_k, tm_k_a, tm_r_k,
                          tm_lnx_g, tm_lnx_b, w_o_a, cv_w, cv_b, cv_ln_g, cv_ln_b, w_o_b, w_out, ln1_g,
                          ln1_b, w_up, ffn_cv_w, ffn_cv_b, w_down, ln2_g, ln2_b)
        mod = mods[l].reshape(MOD_ROWS, 6, d)
        xp, s_l = _path_layer(xp, mod[0:1], s_ctx0, p, bp, lp, False)
        xs, _ = _path_layer(xs, mod[1:1 + bs], _state_to_tiles(state_rwkv[:, l]), p, bs, ls, True)
        new_states.append(s_l)
    return (xp.reshape(bp, lp, d), xs.reshape(bs, ls, d), jnp.stack(new_states, axis=1))
```

```python
import functools
import math

import jax
import jax.numpy as jnp
from jax import lax
from jax.experimental import pallas as pl
from jax.experimental.pallas import tpu as pltpu

D_MODEL = 2048
DEPTH = 4
GRID_W = 64
HEAD_DIM = 64
C_A = D_MODEL // 2
H_A = C_A // HEAD_DIM
C_B = D_MODEL // 2
DECAY_LORA = 64
ICLR_LORA = 64
GATE_LORA = 160
CONV_K = 31
FFN_CONV_K = 3
D_FF = ((8 * D_MODEL // 3 + 127) // 128) * 128
C_RW = 3 * C_A + 2 * DECAY_LORA + 2 * ICLR_LORA + GATE_LORA
ALPHA = (2 * DEPTH) ** 0.25
LN_EPS = 1e-5
GN_EPS = HEAD_DIM * 1e-5

LANES = 128
SUBLANES = 8
HEAD_PAIRS = C_A // LANES
SCAN_T = 128
ROW_OPERANDS = 6
XLU_PAIRS = 4
VMEM_LIMIT = 56 * 1024 * 1024
ROW_BLOCK = 2048
DECAY_SCALE = math.exp(-0.5)
MOD_ROWS = 16

Z_R, Z_K, Z_V, Z_A = 0, C_A, 2 * C_A, 3 * C_A
Z_GA = Z_A + C_B
Z_GB = Z_GA + D_MODEL
Z_B = Z_GB + D_MODEL
Z_LORA = Z_B + C_B
LORA_W = 4 * LANES
C_Z = Z_LORA + LORA_W
FF_PAD = ((D_FF + 4 * LANES - 1) // (4 * LANES)) * (4 * LANES)

BF16 = jnp.bfloat16
F32 = jnp.float32


def _scan_kernel(rf_ref, vf_ref, kkf_ref, wf_ref, kf_ref, bf_ref,
                 rb_ref, vb_ref, kkb_ref, wb_ref, kb_ref, bb_ref, s0_ref, bd_ref,
                 yf_ref, yb_ref, sout_ref, s_scr, vt_scr, y_scr, rows_scr, *, nb):
    j = pl.program_id(1)
    nj = pl.num_programs(1)
    dirs = ((rf_ref, vf_ref, kkf_ref, wf_ref, kf_ref, bf_ref, yf_ref),
            (rb_ref, vb_ref, kkb_ref, wb_ref, kb_ref, bb_ref, yb_ref))
    chains = [(d, n, p) for d in range(2) for n in range(nb) for p in range(HEAD_PAIRS)]
    tile3 = (HEAD_DIM // SUBLANES, SUBLANES, LANES)
    R_, KK_, W_, K_, B_, V_ = range(ROW_OPERANDS)
    lane = lax.broadcasted_iota(jnp.int32, (HEAD_DIM, LANES), 1)
    low_half = lane < HEAD_DIM
    lane_in_head = lane & (HEAD_DIM - 1)

    def cid(d, n, p):
        return (d * nb + n) * HEAD_PAIRS + p

    @pl.when(j == 0)
    def _():
        for d, n, p in chains:
            s_scr[cid(d, n, p)] = s0_ref[n, d, p]

    for d, n, p in chains:
        c = cid(d, n, p)
        y_scr[c] = jnp.zeros((2, HEAD_DIM, LANES), F32)
        if p < XLU_PAIRS:
            continue
        vt = dirs[d][1][n, :, p * LANES:(p + 1) * LANES].T
        h0, h1 = vt[:HEAD_DIM], vt[HEAD_DIM:]
        vt_scr[c, 0] = jnp.where(low_half, h0, pltpu.roll(h1, HEAD_DIM, 1))
        vt_scr[c, 1] = jnp.where(low_half, pltpu.roll(h0, HEAD_DIM, 1), h1)

    for d in range(2):
        for k, src in ((R_, dirs[d][0]), (KK_, dirs[d][2]), (W_, dirs[d][3]), (K_, dirs[d][4]),
                       (B_, dirs[d][5]), (V_, dirs[d][1])):
            for n in range(nb):
                blk = src[0, n] if k in (W_, K_, B_) else src[n]
                rows_scr[d * ROW_OPERANDS + k, n] = blk.reshape(SCAN_T, HEAD_PAIRS, LANES)

    def row(d, k, n, t, p):
        return rows_scr[d * ROW_OPERANDS + k, n, t, pl.ds(p, 1), :]

    def v_columns(d, n, t, p):
        vrep = jnp.broadcast_to(row(d, V_, n, t, p), (LANES, LANES)).T
        return jnp.where(low_half, vrep[:HEAD_DIM], vrep[HEAD_DIM:])

    def weighted(c, r_row, kk_row):
        s = s_scr[c].reshape(tile3)
        return jnp.concatenate(
            [(s * r_row).reshape(HEAD_DIM, LANES), (s * kk_row).reshape(HEAD_DIM, LANES)], axis=1).astype(BF16)

    def step(i, carry):
        ts = (i, SCAN_T - 1 - i)
        tprev = (i - 1, SCAN_T - i)
        lhs = []
        for d, n, p in chains:
            c = cid(d, n, p)
            tp = jnp.clip(tprev[d], 0, SCAN_T - 1)
            lhs.append(weighted(c, row(d, R_, n, tp, p), row(d, KK_, n, ts[d], p)))
        vrow = {}
        for d, n, p in chains:
            if p % 2 or p < XLU_PAIRS:
                continue
            c = cid(d, n, p)
            at_t = lane_in_head == (ts[d] & (HEAD_DIM - 1))
            half = ts[d] // HEAD_DIM
            vrow[c] = vrow[c + 1] = len(lhs) * HEAD_DIM
            lhs.append(jnp.concatenate([jnp.where(at_t, vt_scr[c, half], 0.0),
                                        jnp.where(at_t, vt_scr[c + 1, half], 0.0)], axis=1).astype(BF16))
        res = jnp.dot(jnp.concatenate(lhs, axis=0), bd_ref[...], preferred_element_type=F32)
        for d, n, p in chains:
            c = cid(d, n, p)
            t = ts[d]
            tp = jnp.clip(tprev[d], 0, SCAN_T - 1)
            at_prev = (lane_in_head == (tp & (HEAD_DIM - 1))) & (i > 0)
            hp = tp // HEAD_DIM
            rc = res[c * HEAD_DIM:(c + 1) * HEAD_DIM]
            y_scr[c, hp] = jnp.where(at_prev, rc[:, :LANES], y_scr[c, hp])
            skk = rc[:, LANES:].reshape(tile3)
            if p < XLU_PAIRS:
                vcol = v_columns(d, n, t, p).reshape(tile3)
            else:
                vcol = res[vrow[c]:vrow[c] + HEAD_DIM, (c % 2) * LANES:(c % 2 + 1) * LANES].reshape(tile3)
            s = s_scr[c].reshape(tile3)
            s = s * row(d, W_, n, t, p) - skk * row(d, B_, n, t, p) + vcol * row(d, K_, n, t, p)
            s_scr[c] = s.reshape(HEAD_DIM, LANES)
        return carry

    lax.fori_loop(0, SCAN_T, step, 0)

    tlast = (SCAN_T - 1, 0)
    for d, n, p in chains:
        c = cid(d, n, p)
        r_row = row(d, R_, n, tlast[d], p)
        yl = jnp.dot(weighted(c, r_row, r_row), bd_ref[...], preferred_element_type=F32)[:, :LANES]
        hl = tlast[d] // HEAD_DIM
        at_last = lane_in_head == (tlast[d] & (HEAD_DIM - 1))
        lo = y_scr[c, 0]
        hi = y_scr[c, 1]
        if hl == 0:
            lo = jnp.where(at_last, yl, lo)
        else:
            hi = jnp.where(at_last, yl, hi)
        y0 = jnp.where(low_half, lo, pltpu.roll(hi, HEAD_DIM, 1))
        y1 = jnp.where(low_half, pltpu.roll(lo, HEAD_DIM, 1), hi)
        dirs[d][6][n, :, p * LANES:(p + 1) * LANES] = jnp.concatenate([y0, y1], axis=0).T

    @pl.when(j == nj - 1)
    def _():
        for d, n, p in chains:
            s = s_scr[cid(d, n, p)]
            sout_ref[n, d, 2 * p] = s[:, :HEAD_DIM]
            sout_ref[n, d, 2 * p + 1] = s[:, HEAD_DIM:]


def _head_ones(n):
    i = jnp.arange(n)
    return (i[:, None] // HEAD_DIM == i[None, :] // HEAD_DIM).astype(BF16)


def wkv_scan(r, v, kk, w2, k2, b2, s0):
    bsz, seq, _ = r.shape
    nb = 2
    nj = seq // SCAN_T

    def flat(rev):
        return pl.BlockSpec((nb, SCAN_T, C_A), lambda b, j: (b, nj - 1 - j if rev else j, 0))

    def flat_dir(d):
        return pl.BlockSpec((1, nb, SCAN_T, C_A), lambda b, j: (d, b, nj - 1 - j if d else j, 0))

    state = pl.BlockSpec((nb, 2, HEAD_PAIRS, HEAD_DIM, LANES), lambda b, j: (b, 0, 0, 0, 0))

    nchains = 2 * nb * HEAD_PAIRS
    return pl.pallas_call(
        functools.partial(_scan_kernel, nb=nb),
        name="wkv_scan",
        grid=(bsz // nb, nj),
        in_specs=[flat(0), flat(0), flat(0), flat_dir(0), flat_dir(0), flat_dir(0),
                  flat(1), flat(1), flat(1), flat_dir(1), flat_dir(1), flat_dir(1),
                  state, pl.BlockSpec((2 * LANES, 2 * LANES), lambda b, j: (0, 0))],
        out_specs=[flat(0), flat(1),
                   pl.BlockSpec((nb, 2, H_A, HEAD_DIM, HEAD_DIM), lambda b, j: (b, 0, 0, 0, 0))],
        out_shape=[jax.ShapeDtypeStruct((bsz, seq, C_A), F32),
                   jax.ShapeDtypeStruct((bsz, seq, C_A), F32),
                   jax.ShapeDtypeStruct((bsz, 2, H_A, HEAD_DIM, HEAD_DIM), F32)],
        scratch_shapes=[pltpu.VMEM((nchains, HEAD_DIM, LANES), F32),
                        pltpu.VMEM((nchains, 2, HEAD_DIM, LANES), F32),
                        pltpu.VMEM((nchains, 2, HEAD_DIM, LANES), F32),
                        pltpu.VMEM((2 * ROW_OPERANDS, nb, SCAN_T, HEAD_PAIRS, LANES), F32)],
        compiler_params=pltpu.CompilerParams(
            dimension_semantics=("arbitrary", "arbitrary"),
            vmem_limit_bytes=VMEM_LIMIT),
    )(r, v, kk, w2, k2, b2, r, v, kk, w2, k2, b2, s0, _head_ones(2 * LANES))


def _state_to_tiles(s):
    b = s.shape[0]
    s = s.reshape(b, 2, HEAD_PAIRS, 2, HEAD_DIM, HEAD_DIM)
    return jnp.transpose(s, (0, 1, 2, 4, 3, 5)).reshape(b, 2, HEAD_PAIRS, HEAD_DIM, LANES)


def _rows_in_seq(shape, seq_len):
    return lax.broadcasted_iota(jnp.int32, shape, 0) & (seq_len - 1)


def _shift_rows(x, offset, rs, seq_len):
    rolled = pltpu.roll(x, (-offset) % x.shape[0], 0)
    ok = (rs >= -offset) if offset < 0 else (rs < seq_len - offset)
    return jnp.where(ok, rolled, 0.0)


def _token_shift(x, seq_len, grid):
    rs = _rows_in_seq(x.shape, seq_len)
    col = lax.broadcasted_iota(jnp.int32, x.shape, 1)
    if not grid:
        return jnp.where((col & 1) == 0, _shift_rows(x, -1, rs, seq_len), _shift_rows(x, 1, rs, seq_len))
    gc = rs & (GRID_W - 1)
    left = jnp.where(gc == 0, 0.0, pltpu.roll(x, 1, 0))
    right = jnp.where(gc == GRID_W - 1, 0.0, pltpu.roll(x, x.shape[0] - 1, 0))
    up = _shift_rows(x, -GRID_W, rs, seq_len)
    down = _shift_rows(x, GRID_W, rs, seq_len)
    q = col & 3
    return jnp.where(q == 0, left, jnp.where(q == 1, right, jnp.where(q == 2, up, down)))


def _split_dot(x, ones):
    hi = x.astype(BF16)
    lo = (x - hi.astype(F32)).astype(BF16)
    return (jnp.dot(hi, ones, preferred_element_type=F32) + jnp.dot(lo, ones, preferred_element_type=F32))


def _rwkv_pre_kernel(zr_ref, zk_ref, zv_ref, zl_ref, mur_ref, muk_ref, muv_ref, mul_ref,
                     w0_ref, a0_ref, kkw_ref, kaw_ref, rkw_ref, ww_ref, wa_ref, wg_ref, ones_ref,
                     r_ref, v_ref, kk_ref, g_ref, bonus_ref, w2_ref, k2_ref, b2_ref,
                     twl_scr, al_scr, sgl_scr, *, seq_len, grid):
    def mix(z, mu):
        return z + (_token_shift(z, seq_len, grid) - z) * mu

    @pl.when(pl.program_id(1) == 0)
    def _():
        zl = mix(zl_ref[...], mul_ref[...])
        twl_scr[...] = jnp.tanh(zl[:, :LANES]).astype(BF16)
        al_scr[...] = zl[:, LANES:2 * LANES].astype(BF16)
        sgl_scr[...] = jax.nn.sigmoid(zl[:, 2 * LANES:]).astype(BF16)

    r = mix(zr_ref[...], mur_ref[...])
    k = mix(zk_ref[...], muk_ref[...])
    v = mix(zv_ref[...], muv_ref[...])
    ones = ones_ref[...]
    kx = k * kkw_ref[...]
    kk = kx / jnp.maximum(jnp.sqrt(_split_dot(kx * kx, ones)), 1e-12)
    wd = jnp.dot(twl_scr[...], ww_ref[0], preferred_element_type=F32)
    ad = jnp.dot(al_scr[...], wa_ref[0], preferred_element_type=F32)
    ksum = None
    for d in range(2):
        cols = slice(d * LANES, (d + 1) * LANES)
        w2_ref[d] = jnp.exp(-DECAY_SCALE * jax.nn.sigmoid(wd[:, cols] + w0_ref[d:d + 1, :]))
        a_d = jax.nn.sigmoid(ad[:, cols] + a0_ref[d:d + 1, :])
        k_d = k * (1.0 + (a_d - 1.0) * kaw_ref[...])
        k2_ref[d] = k_d
        b2_ref[d] = kk * a_d
        ksum = k_d if ksum is None else ksum + k_d
    r_ref[...] = r
    v_ref[...] = v
    kk_ref[...] = kk
    g_ref[...] = jnp.dot(sgl_scr[...], wg_ref[0], preferred_element_type=F32)
    bonus_ref[...] = _split_dot(r * ksum * rkw_ref[...], ones) * v


def rwkv_pre(z, mu, p, seq_len, grid):
    t = z.shape[0]
    rb = ROW_BLOCK

    def zcol(off):
        return pl.BlockSpec((rb, LANES), lambda i, q: (i, off + q))

    def vcol(off):
        return pl.BlockSpec((1, LANES), lambda i, q: (0, off + q))

    def pair_vec(rows):
        return pl.BlockSpec((rows, LANES), lambda i, q: (0, q))

    def pair_w(a, b):
        return pl.BlockSpec((1, a, b), lambda i, q: (q, 0, 0))

    lora_blk = Z_LORA // LORA_W
    out1 = pl.BlockSpec((rb, LANES), lambda i, q: (i, q))
    out2 = pl.BlockSpec((2, rb, LANES), lambda i, q: (0, i, q))
    s1 = jax.ShapeDtypeStruct((t, C_A), F32)
    s2 = jax.ShapeDtypeStruct((2, t, C_A), F32)
    return pl.pallas_call(
        functools.partial(_rwkv_pre_kernel, seq_len=seq_len, grid=grid),
        name="rwkv_pre",
        grid=(t // rb, HEAD_PAIRS),
        in_specs=[zcol(Z_R // LANES), zcol(Z_K // LANES), zcol(Z_V // LANES),
                  pl.BlockSpec((rb, LORA_W), lambda i, q: (i, lora_blk)),
                  vcol(Z_R // LANES), vcol(Z_K // LANES), vcol(Z_V // LANES),
                  pl.BlockSpec((1, LORA_W), lambda i, q: (0, lora_blk)),
                  pair_vec(2), pair_vec(2), pair_vec(1), pair_vec(1), pair_vec(1),
                  pair_w(LANES, 2 * LANES), pair_w(LANES, 2 * LANES), pair_w(2 * LANES, LANES),
                  pl.BlockSpec((LANES, LANES), lambda i, q: (0, 0))],
        out_specs=[out1, out1, out1, out1, out1, out2, out2, out2],
        out_shape=[s1, s1, s1, s1, s1, s2, s2, s2],
        scratch_shapes=[pltpu.VMEM((rb, LANES), BF16), pltpu.VMEM((rb, LANES), BF16),
                        pltpu.VMEM((rb, 2 * LANES), BF16)],
        compiler_params=pltpu.CompilerParams(
            dimension_semantics=("arbitrary", "arbitrary"), vmem_limit_bytes=VMEM_LIMIT),
    )(z, z, z, z, mu, mu, mu, mu, p["tm_w0"], p["tm_a0"], p["tm_k_k"], p["tm_k_a"], p["tm_r_k"],
      p["tm_w_b"], p["tm_a_b"], p["tm_g_b"], _head_ones(LANES))


def _glu_conv_kernel(a_ref, b_ref, w_ref, cb_ref, o_ref, *, seq_len):
    u = a_ref[...] * jax.nn.sigmoid(b_ref[...])
    rs = _rows_in_seq(u.shape, seq_len)
    half = (CONV_K - 1) // 2
    acc = cb_ref[...] + u * w_ref[half:half + 1, :]
    for k in range(CONV_K):
        if k != half:
            acc = acc + _shift_rows(u, k - half, rs, seq_len) * w_ref[k:k + 1, :]
    o_ref[...] = acc


def glu_conv(z, cv_w, cv_b, seq_len):
    t = z.shape[0]
    rb, cw = ROW_BLOCK, 2 * LANES
    return pl.pallas_call(
        functools.partial(_glu_conv_kernel, seq_len=seq_len),
        name="glu_conv",
        grid=(t // rb, C_B // cw),
        in_specs=[pl.BlockSpec((rb, cw), lambda i, j: (i, Z_A // cw + j)),
                  pl.BlockSpec((rb, cw), lambda i, j: (i, Z_B // cw + j)),
                  pl.BlockSpec((CONV_K, cw), lambda i, j: (0, j)),
                  pl.BlockSpec((1, cw), lambda i, j: (0, j))],
        out_specs=pl.BlockSpec((rb, cw), lambda i, j: (i, j)),
        out_shape=jax.ShapeDtypeStruct((t, C_B), F32),
        compiler_params=pltpu.CompilerParams(
            dimension_semantics=("arbitrary", "arbitrary"), vmem_limit_bytes=VMEM_LIMIT),
    )(z, z, cv_w, cv_b)


def _ffn_act_kernel(g_ref, v_ref, wg_ref, wv_ref, bg_ref, bv_ref, o_ref, *, seq_len):
    rs = _rows_in_seq(g_ref.shape, seq_len)

    def conv3(x, w_ref, b_ref):
        return (b_ref[...] + _shift_rows(x, -1, rs, seq_len) * w_ref[0:1, :] + x * w_ref[1:2, :]
                + _shift_rows(x, 1, rs, seq_len) * w_ref[2:3, :])

    ug = conv3(g_ref[...].astype(F32), wg_ref, bg_ref)
    uv = conv3(v_ref[...].astype(F32), wv_ref, bv_ref)
    o_ref[...] = (ug * jax.nn.sigmoid(ug) * uv).astype(BF16)


def ffn_act(up, cw_, cb_, seq_len):
    t = up.shape[0]
    rb, cw = ROW_BLOCK, 4 * LANES
    nj = FF_PAD // cw
    return pl.pallas_call(
        functools.partial(_ffn_act_kernel, seq_len=seq_len),
        name="ffn_act",
        grid=(t // rb, nj),
        in_specs=[pl.BlockSpec((rb, cw), lambda i, j: (i, j)),
                  pl.BlockSpec((rb, cw), lambda i, j: (i, nj + j)),
                  pl.BlockSpec((FFN_CONV_K, cw), lambda i, j: (0, j)),
                  pl.BlockSpec((FFN_CONV_K, cw), lambda i, j: (0, nj + j)),
                  pl.BlockSpec((1, cw), lambda i, j: (0, j)),
                  pl.BlockSpec((1, cw), lambda i, j: (0, nj + j))],
        out_specs=pl.BlockSpec((rb, cw), lambda i, j: (i, j)),
        out_shape=jax.ShapeDtypeStruct((t, FF_PAD), BF16),
        compiler_params=pltpu.CompilerParams(
            dimension_semantics=("arbitrary", "arbitrary"), vmem_limit_bytes=VMEM_LIMIT),
    )(up, up, cw_, cw_, cb_, cb_)


def _seq_of_block(tm, seq_len, nseq):
    if nseq == 1:
        return lambda i: 0
    return lambda i: (i * tm) // seq_len


def _mod_mm_kernel(x_ref, mod_ref, w_ref, o_ref, h_scr, *, shift_row, scale_row):
    @pl.when(pl.program_id(1) == 0)
    def _():
        h = x_ref[...] * (1.0 + mod_ref[0, scale_row:scale_row + 1, :]) + mod_ref[0, shift_row:shift_row + 1, :]
        h_scr[...] = h.astype(BF16)

    o_ref[...] = jnp.dot(h_scr[...], w_ref[...], preferred_element_type=F32).astype(o_ref.dtype)


def mod_matmul(x, mod, w, shift_row, scale_row, seq_len, out_dtype):
    t, d = x.shape
    n = w.shape[1]
    tm, tn = 1024, 512
    seq = _seq_of_block(tm, seq_len, mod.shape[0])
    return pl.pallas_call(
        functools.partial(_mod_mm_kernel, shift_row=shift_row, scale_row=scale_row),
        name="mod_matmul",
        grid=(t // tm, n // tn),
        in_specs=[pl.BlockSpec((tm, d), lambda i, j: (i, 0)),
                  pl.BlockSpec((1, 6, d), lambda i, j: (seq(i), 0, 0)),
                  pl.BlockSpec((d, tn), lambda i, j: (0, j))],
        out_specs=pl.BlockSpec((tm, tn), lambda i, j: (i, j)),
        out_shape=jax.ShapeDtypeStruct((t, n), out_dtype),
        scratch_shapes=[pltpu.VMEM((tm, d), BF16)],
        compiler_params=pltpu.CompilerParams(
            dimension_semantics=("arbitrary", "arbitrary"), vmem_limit_bytes=VMEM_LIMIT),
    )(x, mod, w)


def _ln_rows(x, g, b):
    mu = jnp.mean(x, -1, keepdims=True)
    xc = x - mu
    var = jnp.mean(xc * xc, -1, keepdims=True)
    return xc * lax.rsqrt(var + LN_EPS) * g + b


def _mm_resln_kernel(a_ref, w_ref, x_ref, mod_ref, g_ref, b_ref, o_ref, *, gate_row):
    f = jnp.dot(a_ref[...], w_ref[...], preferred_element_type=F32)
    o_ref[...] = _ln_rows(ALPHA * x_ref[...] + mod_ref[0, gate_row:gate_row + 1, :] * f, g_ref[...], b_ref[...])


def mm_resln(a, w, x, mod, gate_row, ln_g, ln_b, seq_len):
    t, k = a.shape
    d = w.shape[1]
    tm = 256
    seq = _seq_of_block(tm, seq_len, mod.shape[0])
    const = lambda i: (0, 0)
    return pl.pallas_call(
        functools.partial(_mm_resln_kernel, gate_row=gate_row),
        name="mm_resln",
        grid=(t // tm,),
        in_specs=[pl.BlockSpec((tm, k), lambda i: (i, 0)),
                  pl.BlockSpec((k, d), const, pipeline_mode=pl.Buffered(1)),
                  pl.BlockSpec((tm, d), lambda i: (i, 0)),
                  pl.BlockSpec((1, 6, d), lambda i: (seq(i), 0, 0)),
                  pl.BlockSpec((1, d), const), pl.BlockSpec((1, d), const)],
        out_specs=pl.BlockSpec((tm, d), lambda i: (i, 0)),
        out_shape=jax.ShapeDtypeStruct((t, d), F32),
        compiler_params=pltpu.CompilerParams(
            dimension_semantics=("arbitrary",), vmem_limit_bytes=VMEM_LIMIT),
    )(a, w, x, mod, ln_g, ln_b)


def _branch_mix_kernel(yf_ref, yb_ref, g_ref, bonus_ref, u_ref, ga_ref, gb_ref,
                       lnxg_ref, lnxb_ref, cvg_ref, cvb_ref, woa_ref, wob_ref, ones_ref, o_ref):
    ones = ones_ref[...]
    cw = ones.shape[0]

    def head_mean(x):
        parts = [_split_dot(x[:, c:c + cw], ones) for c in range(0, C_A, cw)]
        return jnp.concatenate(parts, axis=1) * (1.0 / HEAD_DIM)

    y = yf_ref[...] + yb_ref[...]
    yc = y - head_mean(y)
    yn = yc * lax.rsqrt(head_mean(yc * yc) + GN_EPS) * lnxg_ref[...] + lnxb_ref[...]
    ya = jnp.dot(((yn + bonus_ref[...]) * g_ref[...]).astype(BF16), woa_ref[...], preferred_element_type=F32)
    ub = _ln_rows(u_ref[...], cvg_ref[...], cvb_ref[...])
    ub = ub * jax.nn.sigmoid(ub)
    yb = jnp.dot(ub.astype(BF16), wob_ref[...], preferred_element_type=F32)
    o_ref[...] = (jax.nn.sigmoid(ga_ref[...]) * ya + jax.nn.sigmoid(gb_ref[...]) * yb).astype(BF16)


def branch_mix(yf, yb, g, bonus, uconv, z, p):
    t = yf.shape[0]
    tm = 256
    const = lambda i: (0, 0)
    act = pl.BlockSpec((tm, C_A), lambda i: (i, 0))
    vec = pl.BlockSpec((1, C_A), const)
    wspec = pl.BlockSpec((C_A, D_MODEL), const, pipeline_mode=pl.Buffered(1))
    return pl.pallas_call(
        _branch_mix_kernel,
        name="branch_mix",
        grid=(t // tm,),
        in_specs=[act, act, act, act, act,
                  pl.BlockSpec((tm, D_MODEL), lambda i: (i, Z_GA // D_MODEL)),
                  pl.BlockSpec((tm, D_MODEL), lambda i: (i, Z_GB // D_MODEL)),
                  vec, vec, vec, vec, wspec, wspec,
                  pl.BlockSpec((2 * LANES, 2 * LANES), const)],
        out_specs=pl.BlockSpec((tm, D_MODEL), lambda i: (i, 0)),
        out_shape=jax.ShapeDtypeStruct((t, D_MODEL), BF16),
        compiler_params=pltpu.CompilerParams(
            dimension_semantics=("arbitrary",), vmem_limit_bytes=VMEM_LIMIT),
    )(yf, yb, g, bonus, uconv, z, z, p["tm_lnx_g"], p["tm_lnx_b"], p["cv_ln_g"], p["cv_ln_b"],
      p["w_o_a"], p["w_o_b"], _head_ones(2 * LANES))


def _mod_kernel(c_ref, w_ref, b_ref, o_ref):
    o_ref[0] = jnp.dot(c_ref[...], w_ref[0].astype(BF16), preferred_element_type=F32) + b_ref[0]


def modulation(cond, w_mod, b_mod):
    depth, d, n = w_mod.shape
    tn = 1024
    return pl.pallas_call(
        _mod_kernel,
        name="modulation",
        grid=(depth, n // tn),
        in_specs=[pl.BlockSpec((MOD_ROWS, d), lambda l, j: (0, 0)),
                  pl.BlockSpec((1, d, tn), lambda l, j: (l, 0, j)),
                  pl.BlockSpec((1, 1, tn), lambda l, j: (l, 0, j))],
        out_specs=pl.BlockSpec((1, MOD_ROWS, tn), lambda l, j: (l, 0, j)),
        out_shape=jax.ShapeDtypeStruct((depth, MOD_ROWS, n), F32),
        compiler_params=pltpu.CompilerParams(
            dimension_semantics=("arbitrary", "arbitrary"), vmem_limit_bytes=VMEM_LIMIT),
    )(cond, w_mod, b_mod.reshape(depth, 1, n))


def _path_layer(x, mod, s0, p, bsz, seq_len, grid):
    t = x.shape[0]
    z = mod_matmul(x, mod, p["w_in"], 0, 1, seq_len, F32)
    r, v, kk, g, bonus, w2, k2, b2 = rwkv_pre(z, p["tm_mu"], p, seq_len, grid)
    seq3 = (bsz, seq_len, C_A)
    seq4 = (2, bsz, seq_len, C_A)
    yf, yb, s_out = wkv_scan(r.reshape(seq3), v.reshape(seq3), kk.reshape(seq3),
                             w2.reshape(seq4), k2.reshape(seq4), b2.reshape(seq4), s0)
    uconv = glu_conv(z, p["cv_w"], p["cv_b"], seq_len)
    ymix = branch_mix(yf.reshape(t, C_A), yb.reshape(t, C_A), g, bonus, uconv, z, p)
    x = mm_resln(ymix, p["w_out"], x, mod, 2, p["ln1_g"], p["ln1_b"], seq_len)
    up = mod_matmul(x, mod, p["w_up"], 3, 4, seq_len, BF16)
    act = ffn_act(up, p["ffn_cv_w"], p["ffn_cv_b"], seq_len)
    x = mm_resln(act, p["w_down"], x, mod, 5, p["ln2_g"], p["ln2_b"], seq_len)
    return x, s_out


def _pad_cols(x, n):
    return jnp.pad(x, ((0, 0), (0, n - x.shape[1])))


def _pair_blocks(w):
    return jnp.transpose(w.reshape(w.shape[0], HEAD_PAIRS, LANES), (1, 0, 2))


def _layer_params(l, w_in, tm_mu, tm_w0, tm_w_b, tm_a0, tm_a_b, tm_g_b, tm_k_k, tm_k_a, tm_r_k, tm_lnx_g,
                  tm_lnx_b, w_o_a, cv_w, cv_b, cv_ln_g, cv_ln_b, w_o_b, w_out, ln1_g, ln1_b, w_up,
                  ffn_cv_w, ffn_cv_b, w_down, ln2_g, ln2_b):
    o_lora = 3 * C_A
    o_a = C_RW
    o_b = C_RW + C_B
    o_ga = C_RW + 2 * C_B
    o_gb = o_ga + D_MODEL

    def permute_z(w):
        return jnp.concatenate([w[:, :o_lora], w[:, o_a:o_b], w[:, o_ga:o_gb], w[:, o_gb:], w[:, o_b:o_ga],
                                _pad_cols(w[:, o_lora:o_a], LORA_W)], axis=1)

    def pad_ff(w):
        return jnp.concatenate([_pad_cols(w[:, :D_FF], FF_PAD), _pad_cols(w[:, D_FF:], FF_PAD)], axis=1)

    def dir_blocks(w):
        w0, w1 = _pair_blocks(w[0]), _pair_blocks(w[1])
        zero = jnp.zeros_like(w0)
        return jnp.concatenate([jnp.concatenate([w0, zero], axis=2), jnp.concatenate([zero, w1], axis=2)], axis=1)

    mu = tm_mu[l][None]
    mu_z = jnp.concatenate([mu[:, :o_lora], jnp.zeros((1, Z_LORA - o_lora), F32),
                            _pad_cols(mu[:, o_lora:], LORA_W)], axis=1)
    return {
        "w_in": permute_z(w_in[l]).astype(BF16),
        "tm_mu": mu_z,
        "tm_w0": tm_w0[l], "tm_a0": tm_a0[l],
        "tm_w_b": dir_blocks(tm_w_b[l]).astype(BF16),
        "tm_a_b": dir_blocks(tm_a_b[l]).astype(BF16),
        "tm_g_b": _pair_blocks(jnp.pad(tm_g_b[l], ((0, 2 * LANES - GATE_LORA), (0, 0)))).astype(BF16),
        "tm_k_k": tm_k_k[l][None], "tm_k_a": tm_k_a[l][None], "tm_r_k": tm_r_k[l].reshape(1, C_A),
        "tm_lnx_g": tm_lnx_g[l][None], "tm_lnx_b": tm_lnx_b[l][None],
        "w_o_a": w_o_a[l].astype(BF16), "w_o_b": w_o_b[l].astype(BF16), "w_out": w_out[l].astype(BF16),
        "cv_w": cv_w[l], "cv_b": cv_b[l][None], "cv_ln_g": cv_ln_g[l][None], "cv_ln_b": cv_ln_b[l][None],
        "ln1_g": ln1_g[l][None], "ln1_b": ln1_b[l][None], "ln2_g": ln2_g[l][None], "ln2_b": ln2_b[l][None],
        "w_up": pad_ff(w_up[l]).astype(BF16),
        "ffn_cv_w": pad_ff(ffn_cv_w[l]), "ffn_cv_b": pad_ff(ffn_cv_b[l][None]),
        "w_down": jnp.pad(w_down[l], ((0, FF_PAD - D_FF), (0, 0))).astype(BF16),
    }


def kernel(x_prompt, x_sample, state_rwkv, c, c_ctx, w_mod, b_mod, w_in, tm_mu, tm_w0, tm_w_b, tm_a0,
           tm_a_b, tm_g_b, tm_k_k, tm_k_a, tm_r_k, tm_lnx_g, tm_lnx_b, w_o_a, cv_w, cv_b, cv_ln_g,
           cv_ln_b, w_o_b, w_out, ln1_g, ln1_b, w_up, ffn_cv_w, ffn_cv_b, w_down, ln2_g, ln2_b):
    bp, lp, d = x_prompt.shape
    bs, ls, _ = x_sample.shape
    xp = x_prompt.reshape(bp * lp, d)
    xs = x_sample.reshape(bs * ls, d)
    cond = jnp.concatenate([c_ctx[None], c, jnp.zeros((MOD_ROWS - 1 - bs, d), F32)], axis=0)
    mods = modulation(jax.nn.silu(cond).astype(BF16), w_mod, b_mod)
    s_ctx0 = jnp.zeros((bp, 2, HEAD_PAIRS, HEAD_DIM, LANES), F32)
    new_states = []
    for l in range(DEPTH):
        p = _layer_params(l, w_in, tm_mu, tm_w0, tm_w_b, tm_a0, tm_a_b, tm_g_b, tm_k_k, tm_k_a, tm_r_k,
                          tm_lnx_g, tm_lnx_b, w_o_a, cv_w, cv_b, cv_ln_g, cv_ln_b, w_o_b, w_out, ln1_g,
                          ln1_b, w_up, ffn_cv_w, ffn_cv_b, w_down, ln2_g, ln2_b)
        mod = mods[l].reshape(MOD_ROWS, 6, d)
        xp, s_l = _path_layer(xp, mod[0:1], s_ctx0, p, bp, lp, False)
        xs, _ = _path_layer(xs, mod[1:1 + bs], _state_to_tiles(state_rwkv[:, l]), p, bs, ls, True)
        new_states.append(s_l)
    return (xp.reshape(bp, lp, d), xs.reshape(bs, ls, d), jnp.stack(new_states, axis=1))
```

```python
import functools
import math

import jax
import jax.numpy as jnp
from jax import lax
from jax.experimental import pallas as pl
from jax.experimental.pallas import tpu as pltpu

D_MODEL = 2048
DEPTH = 4
GRID_W = 64
HEAD_DIM = 64
C_A = D_MODEL // 2
H_A = C_A // HEAD_DIM
C_B = D_MODEL // 2
DECAY_LORA = 64
ICLR_LORA = 64
GATE_LORA = 160
CONV_K = 31
FFN_CONV_K = 3
D_FF = ((8 * D_MODEL // 3 + 127) // 128) * 128
C_RW = 3 * C_A + 2 * DECAY_LORA + 2 * ICLR_LORA + GATE_LORA
ALPHA = (2 * DEPTH) ** 0.25
LN_EPS = 1e-5
GN_EPS = HEAD_DIM * 1e-5

LANES = 128
SUBLANES = 8
HEAD_PAIRS = C_A // LANES
SCAN_T = 128
ROW_OPERANDS = 6
XLU_PAIRS = 4
VMEM_LIMIT = 56 * 1024 * 1024
ROW_BLOCK = 2048
DECAY_SCALE = math.exp(-0.5)
MOD_ROWS = 16

Z_R, Z_K, Z_V, Z_A = 0, C_A, 2 * C_A, 3 * C_A
Z_GA = Z_A + C_B
Z_GB = Z_GA + D_MODEL
Z_B = Z_GB + D_MODEL
Z_LORA = Z_B + C_B
LORA_W = 4 * LANES
C_Z = Z_LORA + LORA_W
FF_PAD = ((D_FF + 4 * LANES - 1) // (4 * LANES)) * (4 * LANES)

BF16 = jnp.bfloat16
F32 = jnp.float32


def _scan_kernel(rf_ref, vf_ref, kkf_ref, wf_ref, kf_ref, bf_ref,
                 rb_ref, vb_ref, kkb_ref, wb_ref, kb_ref, bb_ref, s0_ref, bd_ref,
                 yf_ref, yb_ref, sout_ref, s_scr, vt_scr, y_scr, rows_scr, *, nb):
    j = pl.program_id(1)
    nj = pl.num_programs(1)
    dirs = ((rf_ref, vf_ref, kkf_ref, wf_ref, kf_ref, bf_ref, yf_ref),
            (rb_ref, vb_ref, kkb_ref, wb_ref, kb_ref, bb_ref, yb_ref))
    chains = [(d, n, p) for d in range(2) for n in range(nb) for p in range(HEAD_PAIRS)]
    tile3 = (HEAD_DIM // SUBLANES, SUBLANES, LANES)
    R_, KK_, W_, K_, B_, V_ = range(ROW_OPERANDS)
    lane = lax.broadcasted_iota(jnp.int32, (HEAD_DIM, LANES), 1)
    low_half = lane < HEAD_DIM
    lane_in_head = lane & (HEAD_DIM - 1)

    def cid(d, n, p):
        return (d * nb + n) * HEAD_PAIRS + p

    @pl.when(j == 0)
    def _():
        for d, n, p in chains:
            s_scr[cid(d, n, p)] = s0_ref[n, d, p]

    for d, n, p in chains:
        c = cid(d, n, p)
        y_scr[c] = jnp.zeros((2, HEAD_DIM, LANES), F32)
        if p < XLU_PAIRS:
            continue
        vt = dirs[d][1][n, :, p * LANES:(p + 1) * LANES].astype(F32).T
        h0, h1 = vt[:HEAD_DIM], vt[HEAD_DIM:]
        vt_scr[c, 0] = jnp.where(low_half, h0, pltpu.roll(h1, HEAD_DIM, 1))
        vt_scr[c, 1] = jnp.where(low_half, pltpu.roll(h0, HEAD_DIM, 1), h1)

    for d in range(2):
        for k, src in ((R_, dirs[d][0]), (KK_, dirs[d][2]), (W_, dirs[d][3]), (K_, dirs[d][4]),
                       (B_, dirs[d][5]), (V_, dirs[d][1])):
            for n in range(nb):
                blk = src[0, n] if k in (W_, K_, B_) else src[n]
                rows_scr[d * ROW_OPERANDS + k, n] = blk.astype(F32).reshape(SCAN_T, HEAD_PAIRS, LANES)

    def row(d, k, n, t, p):
        return rows_scr[d * ROW_OPERANDS + k, n, t, pl.ds(p, 1), :]

    def v_columns(d, n, t, p):
        vrep = jnp.broadcast_to(row(d, V_, n, t, p), (LANES, LANES)).T
        return jnp.where(low_half, vrep[:HEAD_DIM], vrep[HEAD_DIM:])

    def weighted(c, r_row, kk_row):
        s = s_scr[c].reshape(tile3)
        return jnp.concatenate(
            [(s * r_row).reshape(HEAD_DIM, LANES), (s * kk_row).reshape(HEAD_DIM, LANES)], axis=1).astype(BF16)

    def step(i, carry):
        ts = (i, SCAN_T - 1 - i)
        tprev = (i - 1, SCAN_T - i)
        lhs = []
        for d, n, p in chains:
            c = cid(d, n, p)
            tp = jnp.clip(tprev[d], 0, SCAN_T - 1)
            lhs.append(weighted(c, row(d, R_, n, tp, p), row(d, KK_, n, ts[d], p)))
        vrow = {}
        for d, n, p in chains:
            if p % 2 or p < XLU_PAIRS:
                continue
            c = cid(d, n, p)
            at_t = lane_in_head == (ts[d] & (HEAD_DIM - 1))
            half = ts[d] // HEAD_DIM
            vrow[c] = vrow[c + 1] = len(lhs) * HEAD_DIM
            lhs.append(jnp.concatenate([jnp.where(at_t, vt_scr[c, half], 0.0),
                                        jnp.where(at_t, vt_scr[c + 1, half], 0.0)], axis=1).astype(BF16))
        res = jnp.dot(jnp.concatenate(lhs, axis=0), bd_ref[...], preferred_element_type=F32)
        for d, n, p in chains:
            c = cid(d, n, p)
            t = ts[d]
            tp = jnp.clip(tprev[d], 0, SCAN_T - 1)
            at_prev = (lane_in_head == (tp & (HEAD_DIM - 1))) & (i > 0)
            hp = tp // HEAD_DIM
            rc = res[c * HEAD_DIM:(c + 1) * HEAD_DIM]
            y_scr[c, hp] = jnp.where(at_prev, rc[:, :LANES], y_scr[c, hp])
            skk = rc[:, LANES:].reshape(tile3)
            if p < XLU_PAIRS:
                vcol = v_columns(d, n, t, p).reshape(tile3)
            else:
                vcol = res[vrow[c]:vrow[c] + HEAD_DIM, (c % 2) * LANES:(c % 2 + 1) * LANES].reshape(tile3)
            s = s_scr[c].reshape(tile3)
            s = s * row(d, W_, n, t, p) - skk * row(d, B_, n, t, p) + vcol * row(d, K_, n, t, p)
            s_scr[c] = s.reshape(HEAD_DIM, LANES)
        return carry

    lax.fori_loop(0, SCAN_T, step, 0)

    tlast = (SCAN_T - 1, 0)
    for d, n, p in chains:
        c = cid(d, n, p)
        r_row = row(d, R_, n, tlast[d], p)
        yl = jnp.dot(weighted(c, r_row, r_row), bd_ref[...], preferred_element_type=F32)[:, :LANES]
        hl = tlast[d] // HEAD_DIM
        at_last = lane_in_head == (tlast[d] & (HEAD_DIM - 1))
        lo = y_scr[c, 0]
        hi = y_scr[c, 1]
        if hl == 0:
            lo = jnp.where(at_last, yl, lo)
        else:
            hi = jnp.where(at_last, yl, hi)
        y0 = jnp.where(low_half, lo, pltpu.roll(hi, HEAD_DIM, 1))
        y1 = jnp.where(low_half, pltpu.roll(lo, HEAD_DIM, 1), hi)
        dirs[d][6][n, :, p * LANES:(p + 1) * LANES] = jnp.concatenate([y0, y1], axis=0).T

    @pl.when(j == nj - 1)
    def _():
        for d, n, p in chains:
            s = s_scr[cid(d, n, p)]
            sout_ref[n, d, 2 * p] = s[:, :HEAD_DIM]
            sout_ref[n, d, 2 * p + 1] = s[:, HEAD_DIM:]


def _head_ones(n):
    i = jnp.arange(n)
    return (i[:, None] // HEAD_DIM == i[None, :] // HEAD_DIM).astype(BF16)


def wkv_scan(r, v, kk, w2, k2, b2, s0):
    bsz, seq, _ = r.shape
    nb = 2
    nj = seq // SCAN_T

    def flat(rev):
        return pl.BlockSpec((nb, SCAN_T, C_A), lambda b, j: (b, nj - 1 - j if rev else j, 0))

    def flat_dir(d):
        return pl.BlockSpec((1, nb, SCAN_T, C_A), lambda b, j: (d, b, nj - 1 - j if d else j, 0))

    state = pl.BlockSpec((nb, 2, HEAD_PAIRS, HEAD_DIM, LANES), lambda b, j: (b, 0, 0, 0, 0))

    nchains = 2 * nb * HEAD_PAIRS
    return pl.pallas_call(
        functools.partial(_scan_kernel, nb=nb),
        name="wkv_scan",
        grid=(bsz // nb, nj),
        in_specs=[flat(0), flat(0), flat(0), flat_dir(0), flat_dir(0), flat_dir(0),
                  flat(1), flat(1), flat(1), flat_dir(1), flat_dir(1), flat_dir(1),
                  state, pl.BlockSpec((2 * LANES, 2 * LANES), lambda b, j: (0, 0))],
        out_specs=[flat(0), flat(1),
                   pl.BlockSpec((nb, 2, H_A, HEAD_DIM, HEAD_DIM), lambda b, j: (b, 0, 0, 0, 0))],
        out_shape=[jax.ShapeDtypeStruct((bsz, seq, C_A), F32),
                   jax.ShapeDtypeStruct((bsz, seq, C_A), F32),
                   jax.ShapeDtypeStruct((bsz, 2, H_A, HEAD_DIM, HEAD_DIM), F32)],
        scratch_shapes=[pltpu.VMEM((nchains, HEAD_DIM, LANES), F32),
                        pltpu.VMEM((nchains, 2, HEAD_DIM, LANES), F32),
                        pltpu.VMEM((nchains, 2, HEAD_DIM, LANES), F32),
                        pltpu.VMEM((2 * ROW_OPERANDS, nb, SCAN_T, HEAD_PAIRS, LANES), F32)],
        compiler_params=pltpu.CompilerParams(
            dimension_semantics=("arbitrary", "arbitrary"),
            vmem_limit_bytes=VMEM_LIMIT),
    )(r, v, kk, w2, k2, b2, r, v, kk, w2, k2, b2, s0, _head_ones(2 * LANES))


def _state_to_tiles(s):
    b = s.shape[0]
    s = s.reshape(b, 2, HEAD_PAIRS, 2, HEAD_DIM, HEAD_DIM)
    return jnp.transpose(s, (0, 1, 2, 4, 3, 5)).reshape(b, 2, HEAD_PAIRS, HEAD_DIM, LANES)


def _rows_in_seq(shape, seq_len):
    return lax.broadcasted_iota(jnp.int32, shape, 0) & (seq_len - 1)


def _shift_rows(x, offset, rs, seq_len):
    rolled = pltpu.roll(x, (-offset) % x.shape[0], 0)
    ok = (rs >= -offset) if offset < 0 else (rs < seq_len - offset)
    return jnp.where(ok, rolled, 0.0)


def _token_shift(x, seq_len, grid):
    rs = _rows_in_seq(x.shape, seq_len)
    col = lax.broadcasted_iota(jnp.int32, x.shape, 1)
    if not grid:
        return jnp.where((col & 1) == 0, _shift_rows(x, -1, rs, seq_len), _shift_rows(x, 1, rs, seq_len))
    gc = rs & (GRID_W - 1)
    left = jnp.where(gc == 0, 0.0, pltpu.roll(x, 1, 0))
    right = jnp.where(gc == GRID_W - 1, 0.0, pltpu.roll(x, x.shape[0] - 1, 0))
    up = _shift_rows(x, -GRID_W, rs, seq_len)
    down = _shift_rows(x, GRID_W, rs, seq_len)
    q = col & 3
    return jnp.where(q == 0, left, jnp.where(q == 1, right, jnp.where(q == 2, up, down)))


def _split_dot(x, ones):
    hi = x.astype(BF16)
    lo = (x - hi.astype(F32)).astype(BF16)
    return (jnp.dot(hi, ones, preferred_element_type=F32) + jnp.dot(lo, ones, preferred_element_type=F32))


def _rwkv_pre_kernel(zr_ref, zk_ref, zv_ref, zl_ref, mur_ref, muk_ref, muv_ref, mul_ref,
                     w0_ref, a0_ref, kkw_ref, kaw_ref, rkw_ref, ww_ref, wa_ref, wg_ref, ones_ref,
                     r_ref, v_ref, kk_ref, g_ref, bonus_ref, w2_ref, k2_ref, b2_ref,
                     twl_scr, al_scr, sgl_scr, *, seq_len, grid):
    def mix(z, mu):
        return z + (_token_shift(z, seq_len, grid) - z) * mu

    @pl.when(pl.program_id(1) == 0)
    def _():
        zl = mix(zl_ref[...].astype(F32), mul_ref[...])
        twl_scr[...] = jnp.tanh(zl[:, :LANES]).astype(BF16)
        al_scr[...] = zl[:, LANES:2 * LANES].astype(BF16)
        sgl_scr[...] = jax.nn.sigmoid(zl[:, 2 * LANES:]).astype(BF16)

    r = mix(zr_ref[...].astype(F32), mur_ref[...])
    k = mix(zk_ref[...].astype(F32), muk_ref[...])
    v = mix(zv_ref[...].astype(F32), muv_ref[...])
    ones = ones_ref[...]
    kx = k * kkw_ref[...]
    kk = kx / jnp.maximum(jnp.sqrt(_split_dot(kx * kx, ones)), 1e-12)
    wd = jnp.dot(twl_scr[...], ww_ref[0], preferred_element_type=F32)
    ad = jnp.dot(al_scr[...], wa_ref[0], preferred_element_type=F32)
    ksum = None
    for d in range(2):
        cols = slice(d * LANES, (d + 1) * LANES)
        w2_ref[d] = jnp.exp(-DECAY_SCALE * jax.nn.sigmoid(wd[:, cols] + w0_ref[d:d + 1, :]))
        a_d = jax.nn.sigmoid(ad[:, cols] + a0_ref[d:d + 1, :])
        k_d = k * (1.0 + (a_d - 1.0) * kaw_ref[...])
        k2_ref[d] = k_d.astype(k2_ref.dtype)
        b2_ref[d] = (kk * a_d).astype(b2_ref.dtype)
        ksum = k_d if ksum is None else ksum + k_d
    r_ref[...] = r.astype(r_ref.dtype)
    v_ref[...] = v.astype(v_ref.dtype)
    kk_ref[...] = kk.astype(kk_ref.dtype)
    g_ref[...] = jnp.dot(sgl_scr[...], wg_ref[0], preferred_element_type=F32)
    bonus_ref[...] = _split_dot(r * ksum * rkw_ref[...], ones) * v


def rwkv_pre(z, mu, p, seq_len, grid):
    t = z.shape[0]
    rb = ROW_BLOCK

    def zcol(off):
        return pl.BlockSpec((rb, LANES), lambda i, q: (i, off + q))

    def vcol(off):
        return pl.BlockSpec((1, LANES), lambda i, q: (0, off + q))

    def pair_vec(rows):
        return pl.BlockSpec((rows, LANES), lambda i, q: (0, q))

    def pair_w(a, b):
        return pl.BlockSpec((1, a, b), lambda i, q: (q, 0, 0))

    lora_blk = Z_LORA // LORA_W
    out1 = pl.BlockSpec((rb, LANES), lambda i, q: (i, q))
    out2 = pl.BlockSpec((2, rb, LANES), lambda i, q: (0, i, q))
    s1 = jax.ShapeDtypeStruct((t, C_A), F32)
    s2 = jax.ShapeDtypeStruct((2, t, C_A), F32)
    h1 = jax.ShapeDtypeStruct((t, C_A), BF16)
    h2 = jax.ShapeDtypeStruct((2, t, C_A), BF16)
    return pl.pallas_call(
        functools.partial(_rwkv_pre_kernel, seq_len=seq_len, grid=grid),
        name="rwkv_pre",
        grid=(t // rb, HEAD_PAIRS),
        in_specs=[zcol(Z_R // LANES), zcol(Z_K // LANES), zcol(Z_V // LANES),
                  pl.BlockSpec((rb, LORA_W), lambda i, q: (i, lora_blk)),
                  vcol(Z_R // LANES), vcol(Z_K // LANES), vcol(Z_V // LANES),
                  pl.BlockSpec((1, LORA_W), lambda i, q: (0, lora_blk)),
                  pair_vec(2), pair_vec(2), pair_vec(1), pair_vec(1), pair_vec(1),
                  pair_w(LANES, 2 * LANES), pair_w(LANES, 2 * LANES), pair_w(2 * LANES, LANES),
                  pl.BlockSpec((LANES, LANES), lambda i, q: (0, 0))],
        out_specs=[out1, out1, out1, out1, out1, out2, out2, out2],
        out_shape=[h1, h1, h1, s1, s1, s2, h2, h2],
        scratch_shapes=[pltpu.VMEM((rb, LANES), BF16), pltpu.VMEM((rb, LANES), BF16),
                        pltpu.VMEM((rb, 2 * LANES), BF16)],
        compiler_params=pltpu.CompilerParams(
            dimension_semantics=("arbitrary", "arbitrary"), vmem_limit_bytes=VMEM_LIMIT),
    )(z, z, z, z, mu, mu, mu, mu, p["tm_w0"], p["tm_a0"], p["tm_k_k"], p["tm_k_a"], p["tm_r_k"],
      p["tm_w_b"], p["tm_a_b"], p["tm_g_b"], _head_ones(LANES))


def _glu_conv_kernel(a_ref, b_ref, w_ref, cb_ref, o_ref, *, seq_len):
    u = a_ref[...].astype(F32) * jax.nn.sigmoid(b_ref[...].astype(F32))
    rs = _rows_in_seq(u.shape, seq_len)
    half = (CONV_K - 1) // 2
    acc = cb_ref[...] + u * w_ref[half:half + 1, :]
    for k in range(CONV_K):
        if k != half:
            acc = acc + _shift_rows(u, k - half, rs, seq_len) * w_ref[k:k + 1, :]
    o_ref[...] = acc


def glu_conv(z, cv_w, cv_b, seq_len):
    t = z.shape[0]
    rb, cw = ROW_BLOCK, 2 * LANES
    return pl.pallas_call(
        functools.partial(_glu_conv_kernel, seq_len=seq_len),
        name="glu_conv",
        grid=(t // rb, C_B // cw),
        in_specs=[pl.BlockSpec((rb, cw), lambda i, j: (i, Z_A // cw + j)),
                  pl.BlockSpec((rb, cw), lambda i, j: (i, Z_B // cw + j)),
                  pl.BlockSpec((CONV_K, cw), lambda i, j: (0, j)),
                  pl.BlockSpec((1, cw), lambda i, j: (0, j))],
        out_specs=pl.BlockSpec((rb, cw), lambda i, j: (i, j)),
        out_shape=jax.ShapeDtypeStruct((t, C_B), F32),
        compiler_params=pltpu.CompilerParams(
            dimension_semantics=("arbitrary", "arbitrary"), vmem_limit_bytes=VMEM_LIMIT),
    )(z, z, cv_w, cv_b)


def _ffn_act_kernel(g_ref, v_ref, wg_ref, wv_ref, bg_ref, bv_ref, o_ref, *, seq_len):
    rs = _rows_in_seq(g_ref.shape, seq_len)

    def conv3(x, w_ref, b_ref):
        return (b_ref[...] + _shift_rows(x, -1, rs, seq_len) * w_ref[0:1, :] + x * w_ref[1:2, :]
                + _shift_rows(x, 1, rs, seq_len) * w_ref[2:3, :])

    ug = conv3(g_ref[...].astype(F32), wg_ref, bg_ref)
    uv = conv3(v_ref[...].astype(F32), wv_ref, bv_ref)
    o_ref[...] = (ug * jax.nn.sigmoid(ug) * uv).astype(BF16)


def ffn_act(up, cw_, cb_, seq_len):
    t = up.shape[0]
    rb, cw = ROW_BLOCK, 4 * LANES
    nj = FF_PAD // cw
    return pl.pallas_call(
        functools.partial(_ffn_act_kernel, seq_len=seq_len),
        name="ffn_act",
        grid=(t // rb, nj),
        in_specs=[pl.BlockSpec((rb, cw), lambda i, j: (i, j)),
                  pl.BlockSpec((rb, cw), lambda i, j: (i, nj + j)),
                  pl.BlockSpec((FFN_CONV_K, cw), lambda i, j: (0, j)),
                  pl.BlockSpec((FFN_CONV_K, cw), lambda i, j: (0, nj + j)),
                  pl.BlockSpec((1, cw), lambda i, j: (0, j)),
                  pl.BlockSpec((1, cw), lambda i, j: (0, nj + j))],
        out_specs=pl.BlockSpec((rb, cw), lambda i, j: (i, j)),
        out_shape=jax.ShapeDtypeStruct((t, FF_PAD), BF16),
        compiler_params=pltpu.CompilerParams(
            dimension_semantics=("arbitrary", "arbitrary"), vmem_limit_bytes=VMEM_LIMIT),
    )(up, up, cw_, cw_, cb_, cb_)


def _seq_of_block(tm, seq_len, nseq):
    if nseq == 1:
        return lambda i: 0
    return lambda i: (i * tm) // seq_len


def _mod_mm_kernel(x_ref, mod_ref, w_ref, o_ref, h_scr, *, shift_row, scale_row):
    @pl.when(pl.program_id(1) == 0)
    def _():
        h = x_ref[...] * (1.0 + mod_ref[0, scale_row:scale_row + 1, :]) + mod_ref[0, shift_row:shift_row + 1, :]
        h_scr[...] = h.astype(BF16)

    o_ref[...] = jnp.dot(h_scr[...], w_ref[...], preferred_element_type=F32).astype(o_ref.dtype)


def mod_matmul(x, mod, w, shift_row, scale_row, seq_len, out_dtype):
    t, d = x.shape
    n = w.shape[1]
    tm, tn = 1024, 512
    seq = _seq_of_block(tm, seq_len, mod.shape[0])
    return pl.pallas_call(
        functools.partial(_mod_mm_kernel, shift_row=shift_row, scale_row=scale_row),
        name="mod_matmul",
        grid=(t // tm, n // tn),
        in_specs=[pl.BlockSpec((tm, d), lambda i, j: (i, 0)),
                  pl.BlockSpec((1, 6, d), lambda i, j: (seq(i), 0, 0)),
                  pl.BlockSpec((d, tn), lambda i, j: (0, j))],
        out_specs=pl.BlockSpec((tm, tn), lambda i, j: (i, j)),
        out_shape=jax.ShapeDtypeStruct((t, n), out_dtype),
        scratch_shapes=[pltpu.VMEM((tm, d), BF16)],
        compiler_params=pltpu.CompilerParams(
            dimension_semantics=("arbitrary", "arbitrary"), vmem_limit_bytes=VMEM_LIMIT),
    )(x, mod, w)


def _ln_rows(x, g, b):
    mu = jnp.mean(x, -1, keepdims=True)
    xc = x - mu
    var = jnp.mean(xc * xc, -1, keepdims=True)
    return xc * lax.rsqrt(var + LN_EPS) * g + b


def _mm_resln_kernel(a_ref, w_ref, x_ref, mod_ref, g_ref, b_ref, o_ref, *, gate_row):
    f = jnp.dot(a_ref[...], w_ref[...], preferred_element_type=F32)
    o_ref[...] = _ln_rows(ALPHA * x_ref[...] + mod_ref[0, gate_row:gate_row + 1, :] * f, g_ref[...], b_ref[...])


def mm_resln(a, w, x, mod, gate_row, ln_g, ln_b, seq_len):
    t, k = a.shape
    d = w.shape[1]
    tm = 256
    seq = _seq_of_block(tm, seq_len, mod.shape[0])
    const = lambda i: (0, 0)
    return pl.pallas_call(
        functools.partial(_mm_resln_kernel, gate_row=gate_row),
        name="mm_resln",
        grid=(t // tm,),
        in_specs=[pl.BlockSpec((tm, k), lambda i: (i, 0)),
                  pl.BlockSpec((k, d), const, pipeline_mode=pl.Buffered(1)),
                  pl.BlockSpec((tm, d), lambda i: (i, 0)),
                  pl.BlockSpec((1, 6, d), lambda i: (seq(i), 0, 0)),
                  pl.BlockSpec((1, d), const), pl.BlockSpec((1, d), const)],
        out_specs=pl.BlockSpec((tm, d), lambda i: (i, 0)),
        out_shape=jax.ShapeDtypeStruct((t, d), F32),
        compiler_params=pltpu.CompilerParams(
            dimension_semantics=("arbitrary",), vmem_limit_bytes=VMEM_LIMIT),
    )(a, w, x, mod, ln_g, ln_b)


def _branch_mix_kernel(yf_ref, yb_ref, g_ref, bonus_ref, u_ref, ga_ref, gb_ref,
                       lnxg_ref, lnxb_ref, cvg_ref, cvb_ref, woa_ref, wob_ref, ones_ref, o_ref):
    ones = ones_ref[...]
    cw = ones.shape[0]

    def head_mean(x):
        parts = [_split_dot(x[:, c:c + cw], ones) for c in range(0, C_A, cw)]
        return jnp.concatenate(parts, axis=1) * (1.0 / HEAD_DIM)

    y = yf_ref[...] + yb_ref[...]
    yc = y - head_mean(y)
    yn = yc * lax.rsqrt(head_mean(yc * yc) + GN_EPS) * lnxg_ref[...] + lnxb_ref[...]
    ya = jnp.dot(((yn + bonus_ref[...]) * g_ref[...]).astype(BF16), woa_ref[...], preferred_element_type=F32)
    ub = _ln_rows(u_ref[...], cvg_ref[...], cvb_ref[...])
    ub = ub * jax.nn.sigmoid(ub)
    yb = jnp.dot(ub.astype(BF16), wob_ref[...], preferred_element_type=F32)
    o_ref[...] = (jax.nn.sigmoid(ga_ref[...].astype(F32)) * ya
                  + jax.nn.sigmoid(gb_ref[...].astype(F32)) * yb).astype(BF16)


def branch_mix(yf, yb, g, bonus, uconv, z, p):
    t = yf.shape[0]
    tm = 256
    const = lambda i: (0, 0)
    act = pl.BlockSpec((tm, C_A), lambda i: (i, 0))
    vec = pl.BlockSpec((1, C_A), const)
    wspec = pl.BlockSpec((C_A, D_MODEL), const, pipeline_mode=pl.Buffered(1))
    return pl.pallas_call(
        _branch_mix_kernel,
        name="branch_mix",
        grid=(t // tm,),
        in_specs=[act, act, act, act, act,
                  pl.BlockSpec((tm, D_MODEL), lambda i: (i, Z_GA // D_MODEL)),
                  pl.BlockSpec((tm, D_MODEL), lambda i: (i, Z_GB // D_MODEL)),
                  vec, vec, vec, vec, wspec, wspec,
                  pl.BlockSpec((2 * LANES, 2 * LANES), const)],
        out_specs=pl.BlockSpec((tm, D_MODEL), lambda i: (i, 0)),
        out_shape=jax.ShapeDtypeStruct((t, D_MODEL), BF16),
        compiler_params=pltpu.CompilerParams(
            dimension_semantics=("arbitrary",), vmem_limit_bytes=VMEM_LIMIT),
    )(yf, yb, g, bonus, uconv, z, z, p["tm_lnx_g"], p["tm_lnx_b"], p["cv_ln_g"], p["cv_ln_b"],
      p["w_o_a"], p["w_o_b"], _head_ones(2 * LANES))


def _mod_kernel(c_ref, w_ref, b_ref, o_ref):
    o_ref[0] = jnp.dot(c_ref[...], w_ref[0].astype(BF16), preferred_element_type=F32) + b_ref[0]


def modulation(cond, w_mod, b_mod):
    depth, d, n = w_mod.shape
    tn = 1024
    return pl.pallas_call(
        _mod_kernel,
        name="modulation",
        grid=(depth, n // tn),
        in_specs=[pl.BlockSpec((MOD_ROWS, d), lambda l, j: (0, 0)),
                  pl.BlockSpec((1, d, tn), lambda l, j: (l, 0, j)),
                  pl.BlockSpec((1, 1, tn), lambda l, j: (l, 0, j))],
        out_specs=pl.BlockSpec((1, MOD_ROWS, tn), lambda l, j: (l, 0, j)),
        out_shape=jax.ShapeDtypeStruct((depth, MOD_ROWS, n), F32),
        compiler_params=pltpu.CompilerParams(
            dimension_semantics=("arbitrary", "arbitrary"), vmem_limit_bytes=VMEM_LIMIT),
    )(cond, w_mod, b_mod.reshape(depth, 1, n))


def _path_layer(x, mod, s0, p, bsz, seq_len, grid):
    t = x.shape[0]
    z = mod_matmul(x, mod, p["w_in"], 0, 1, seq_len, BF16)
    r, v, kk, g, bonus, w2, k2, b2 = rwkv_pre(z, p["tm_mu"], p, seq_len, grid)
    seq3 = (bsz, seq_len, C_A)
    seq4 = (2, bsz, seq_len, C_A)
    yf, yb, s_out = wkv_scan(r.reshape(seq3), v.reshape(seq3), kk.reshape(seq3),
                             w2.reshape(seq4), k2.reshape(seq4), b2.reshape(seq4), s0)
    uconv = glu_conv(z, p["cv_w"], p["cv_b"], seq_len)
    ymix = branch_mix(yf.reshape(t, C_A), yb.reshape(t, C_A), g, bonus, uconv, z, p)
    x = mm_resln(ymix, p["w_out"], x, mod, 2, p["ln1_g"], p["ln1_b"], seq_len)
    up = mod_matmul(x, mod, p["w_up"], 3, 4, seq_len, BF16)
    act = ffn_act(up, p["ffn_cv_w"], p["ffn_cv_b"], seq_len)
    x = mm_resln(act, p["w_down"], x, mod, 5, p["ln2_g"], p["ln2_b"], seq_len)
    return x, s_out


def _pad_cols(x, n):
    return jnp.pad(x, ((0, 0), (0, n - x.shape[1])))


def _pair_blocks(w):
    return jnp.transpose(w.reshape(w.shape[0], HEAD_PAIRS, LANES), (1, 0, 2))


def _layer_params(l, w_in, tm_mu, tm_w0, tm_w_b, tm_a0, tm_a_b, tm_g_b, tm_k_k, tm_k_a, tm_r_k, tm_lnx_g,
                  tm_lnx_b, w_o_a, cv_w, cv_b, cv_ln_g, cv_ln_b, w_o_b, w_out, ln1_g, ln1_b, w_up,
                  ffn_cv_w, ffn_cv_b, w_down, ln2_g, ln2_b):
    o_lora = 3 * C_A
    o_a = C_RW
    o_b = C_RW + C_B
    o_ga = C_RW + 2 * C_B
    o_gb = o_ga + D_MODEL

    def permute_z(w):
        return jnp.concatenate([w[:, :o_lora], w[:, o_a:o_b], w[:, o_ga:o_gb], w[:, o_gb:], w[:, o_b:o_ga],
                                _pad_cols(w[:, o_lora:o_a], LORA_W)], axis=1)

    def pad_ff(w):
        return jnp.concatenate([_pad_cols(w[:, :D_FF], FF_PAD), _pad_cols(w[:, D_FF:], FF_PAD)], axis=1)

    def dir_blocks(w):
        w0, w1 = _pair_blocks(w[0]), _pair_blocks(w[1])
        zero = jnp.zeros_like(w0)
        return jnp.concatenate([jnp.concatenate([w0, zero], axis=2), jnp.concatenate([zero, w1], axis=2)], axis=1)

    mu = tm_mu[l][None]
    mu_z = jnp.concatenate([mu[:, :o_lora], jnp.zeros((1, Z_LORA - o_lora), F32),
                            _pad_cols(mu[:, o_lora:], LORA_W)], axis=1)
    return {
        "w_in": permute_z(w_in[l]).astype(BF16),
        "tm_mu": mu_z,
        "tm_w0": tm_w0[l], "tm_a0": tm_a0[l],
        "tm_w_b": dir_blocks(tm_w_b[l]).astype(BF16),
        "tm_a_b": dir_blocks(tm_a_b[l]).astype(BF16),
        "tm_g_b": _pair_blocks(jnp.pad(tm_g_b[l], ((0, 2 * LANES - GATE_LORA), (0, 0)))).astype(BF16),
        "tm_k_k": tm_k_k[l][None], "tm_k_a": tm_k_a[l][None], "tm_r_k": tm_r_k[l].reshape(1, C_A),
        "tm_lnx_g": tm_lnx_g[l][None], "tm_lnx_b": tm_lnx_b[l][None],
        "w_o_a": w_o_a[l].astype(BF16), "w_o_b": w_o_b[l].astype(BF16), "w_out": w_out[l].astype(BF16),
        "cv_w": cv_w[l], "cv_b": cv_b[l][None], "cv_ln_g": cv_ln_g[l][None], "cv_ln_b": cv_ln_b[l][None],
        "ln1_g": ln1_g[l][None], "ln1_b": ln1_b[l][None], "ln2_g": ln2_g[l][None], "ln2_b": ln2_b[l][None],
        "w_up": pad_ff(w_up[l]).astype(BF16),
        "ffn_cv_w": pad_ff(ffn_cv_w[l]), "ffn_cv_b": pad_ff(ffn_cv_b[l][None]),
        "w_down": jnp.pad(w_down[l], ((0, FF_PAD - D_FF), (0, 0))).astype(BF16),
    }


def kernel(x_prompt, x_sample, state_rwkv, c, c_ctx, w_mod, b_mod, w_in, tm_mu, tm_w0, tm_w_b, tm_a0,
           tm_a_b, tm_g_b, tm_k_k, tm_k_a, tm_r_k, tm_lnx_g, tm_lnx_b, w_o_a, cv_w, cv_b, cv_ln_g,
           cv_ln_b, w_o_b, w_out, ln1_g, ln1_b, w_up, ffn_cv_w, ffn_cv_b, w_down, ln2_g, ln2_b):
    bp, lp, d = x_prompt.shape
    bs, ls, _ = x_sample.shape
    xp = x_prompt.reshape(bp * lp, d)
    xs = x_sample.reshape(bs * ls, d)
    cond = jnp.concatenate([c_ctx[None], c, jnp.zeros((MOD_ROWS - 1 - bs, d), F32)], axis=0)
    mods = modulation(jax.nn.silu(cond).astype(BF16), w_mod, b_mod)
    s_ctx0 = jnp.zeros((bp, 2, HEAD_PAIRS, HEAD_DIM, LANES), F32)
    new_states = []
    for l in range(DEPTH):
        p = _layer_params(l, w_in, tm_mu, tm_w0, tm_w_b, tm_a0, tm_a_b, tm_g_b, tm_k_k, tm_k_a, tm_r_k,
                          tm_lnx_g, tm_lnx_b, w_o_a, cv_w, cv_b, cv_ln_g, cv_ln_b, w_o_b, w_out, ln1_g,
                          ln1_b, w_up, ffn_cv_w, ffn_cv_b, w_down, ln2_g, ln2_b)
        mod = mods[l].reshape(MOD_ROWS, 6, d)
        xp, s_l = _path_layer(xp, mod[0:1], s_ctx0, p, bp, lp, False)
        xs, _ = _path_layer(xs, mod[1:1 + bs], _state_to_tiles(state_rwkv[:, l]), p, bs, ls, True)
        new_states.append(s_l)
    return (xp.reshape(bp, lp, d), xs.reshape(bs, ls, d), jnp.stack(new_states, axis=1))
```
